```python
import math
import jax, jax.numpy as jnp
from jax import lax
import numpy as np

D_MODEL = 1024
BATCH = 16
SEQ = 2048
DEPTH = 2
DEC_BATCH = 128
DEC_SEQ = 1
PAST_LEN = 16384
PAGE_SIZE = 128

N_META = 16
HEAD_DIM = 64
A_HEADS = D_MODEL // (4 * HEAD_DIM)
A_WIDTH = A_HEADS * HEAD_DIM
IDX_HEADS = 8
IDX_DIM = 64
TOPK_MAX = 256
B_HEADS = D_MODEL // (4 * HEAD_DIM)
B_KV_HEADS = 2
B_WIDTH = B_HEADS * HEAD_DIM
C_HEADS = D_MODEL // (2 * HEAD_DIM)
C_NOPE_DIM = 64
C_ROPE_DIM = 32
C_V_DIM = HEAD_DIM
C_WIDTH = C_HEADS * C_V_DIM
C_Q_RANK = D_MODEL // 4
C_KV_RANK = D_MODEL // 8
MIX_WIDTH = A_WIDTH + B_WIDTH + C_WIDTH
N_BUCKETS = 32
MAX_DISTANCE = 128
ROPE_THETA = 10000.0
Q_BLOCK = 128
DN_ALPHA = (2.0 * DEPTH) ** 0.25
DN_BETA = (8.0 * DEPTH) ** -0.25
LN_EPS = 1e-5
RMS_EPS = 1e-6

IN_SPLITS = (
    ("a_q", A_WIDTH), ("a_k", HEAD_DIM), ("a_v", HEAD_DIM),
    ("a_iq", IDX_HEADS * IDX_DIM), ("a_ik", IDX_DIM), ("a_iw", IDX_HEADS), ("a_g", A_WIDTH),
    ("b_q", B_WIDTH), ("b_k", B_KV_HEADS * HEAD_DIM), ("b_v", B_KV_HEADS * HEAD_DIM), ("b_g", B_WIDTH),
    ("c_q", C_Q_RANK), ("c_kv", C_KV_RANK), ("c_kr", C_ROPE_DIM), ("c_g", C_WIDTH),
)
IN_WIDTH = sum(w for _, w in IN_SPLITS)

kernel_name = "hybrid_dsa_stickbreak_mla_step"


def _layernorm(x, g, b):
    xf = x.astype(jnp.float32)
    mu = jnp.mean(xf, -1, keepdims=True)
    var = jnp.mean(jnp.square(xf - mu), -1, keepdims=True)
    return ((xf - mu) * lax.rsqrt(var + LN_EPS) * g + b).astype(x.dtype)


def _rms(x, g):
    xf = x.astype(jnp.float32)
    return (xf * lax.rsqrt(jnp.mean(jnp.square(xf), -1, keepdims=True) + RMS_EPS) * g).astype(x.dtype)


def _rope(x, pos):
    half = x.shape[-1] // 2
    freq = ROPE_THETA ** (-jnp.arange(half, dtype=jnp.float32) / half)
    ang = pos.astype(jnp.float32)[:, None] * freq
    ang = ang.reshape((1, ang.shape[0]) + (1,) * (x.ndim - 3) + (half,))
    c, s = jnp.cos(ang), jnp.sin(ang)
    xf = x.astype(jnp.float32)
    x1, x2 = xf[..., :half], xf[..., half:]
    return jnp.concatenate([x1 * c - x2 * s, x1 * s + x2 * c], -1).astype(x.dtype)


def _t5_bucket(dist):
    max_exact = N_BUCKETS // 2
    large = max_exact + (jnp.log(jnp.maximum(dist, 1).astype(jnp.float32) / max_exact)
                         / math.log(MAX_DISTANCE / max_exact) * (N_BUCKETS - max_exact)).astype(jnp.int32)
    large = jnp.minimum(large, N_BUCKETS - 1)
    return jnp.where(dist < max_exact, dist, large)


def _project(x, pos, w_in, c_q_norm, c_kv_norm, w_uq):
    bn, t = x.shape[:2]
    h = x @ w_in
    p, off = {}, 0
    for name, width in IN_SPLITS:
        p[name] = h[..., off:off + width]
        off += width
    out = {
        "a_q": p["a_q"].reshape(bn, t, A_HEADS, HEAD_DIM),
        "a_kv": jnp.stack([p["a_k"], p["a_v"]], axis=2),
        "a_iq": p["a_iq"].reshape(bn, t, IDX_HEADS, IDX_DIM),
        "a_ik": p["a_ik"],
        "a_iw": p["a_iw"] * IDX_HEADS ** -0.5,
        "b_q": p["b_q"].reshape(bn, t, B_HEADS, HEAD_DIM),
        "b_kv": jnp.stack([p["b_k"].reshape(bn, t, B_KV_HEADS, HEAD_DIM),
                           p["b_v"].reshape(bn, t, B_KV_HEADS, HEAD_DIM)], axis=2),
        "a_g": p["a_g"], "b_g": p["b_g"], "c_g": p["c_g"],
    }
    cq = (_rms(p["c_q"], c_q_norm) @ w_uq).reshape(bn, t, C_HEADS, C_NOPE_DIM + C_ROPE_DIM)
    out["c_qn"] = cq[..., :C_NOPE_DIM]
    out["c_qr"] = _rope(cq[..., C_NOPE_DIM:], pos)
    out["c_row"] = jnp.concatenate([_rms(p["c_kv"], c_kv_norm), _rope(p["c_kr"], pos)], -1)
    return out


def _seg_sum(eq, w, segs):
    out, off = None, 0
    for s in segs:
        n = s.shape[1]
        o = jnp.einsum(eq, w[..., off:off + n].astype(s.dtype), s)
        out = o if out is None else out + o
        off += n
    return out


def _indexer_select(iq, iw, ik_segs, q_pos, k_pos, n_sel):
    s = jnp.concatenate([jnp.einsum('bqhd,bsd->bhqs', iq, ik) for ik in ik_segs], -1).astype(jnp.float32)
    score = jnp.einsum('bhqs,bqh->bqs', jax.nn.relu(s) * IDX_DIM ** -0.5, iw.astype(jnp.float32))
    score = jnp.where(k_pos[None, :] <= q_pos[:, None], score, -jnp.inf)
    return lax.top_k(score, n_sel)[1]


def _sparse_attend(q, kv_sel, sel, q_pos, rel_bias):
    dist = q_pos[None, :, None] - sel
    bias = rel_bias[_t5_bucket(jnp.maximum(dist, 0))].astype(jnp.float32)
    logits = (jnp.einsum('bqhd,bqkd->bhqk', q, kv_sel[..., 0, :]).astype(jnp.float32) * HEAD_DIM ** -0.5
              + jnp.transpose(bias, (0, 3, 1, 2)))
    logits = jnp.where((dist >= 0)[:, None], logits, -jnp.inf)
    pr = jax.nn.softmax(logits, axis=-1).astype(kv_sel.dtype)
    return jnp.einsum('bhqk,bqkd->bqhd', pr, kv_sel[..., 1, :])


def _stick_breaking(q, kv_segs, q_pos, k_pos):
    bn, nq = q.shape[:2]
    qg = q.reshape(bn, nq, B_KV_HEADS, B_HEADS // B_KV_HEADS, HEAD_DIM)
    z = jnp.concatenate([jnp.einsum('bqgrd,bsgd->bgrqs', qg, kv[:, :, 0]) for kv in kv_segs], -1)
    z = z.astype(jnp.float32) * HEAD_DIM ** -0.5
    m = k_pos[None, :] < q_pos[:, None]
    lneg = jnp.where(m, jax.nn.log_sigmoid(-z), 0.0)
    rc = lax.cumsum(lneg, axis=z.ndim - 1, reverse=True) - lneg
    w = jnp.where(m, jnp.exp(jax.nn.log_sigmoid(z) + rc), 0.0)
    out = _seg_sum('bgrqs,bsgd->bqgrd', w, [kv[:, :, 1] for kv in kv_segs])
    return out.reshape(bn, nq, B_HEADS, HEAD_DIM)


def _mla_attend(q_nope, q_rope, c_segs, q_pos, k_pos, w_uk, w_uv):
    q_lat = jnp.einsum('bqhd,chd->bqhc', q_nope, w_uk)
    logits = jnp.concatenate([
        jnp.einsum('bqhc,bsc->bhqs', q_lat, c[..., :C_KV_RANK])
        + jnp.einsum('bqhr,bsr->bhqs', q_rope, c[..., C_KV_RANK:]) for c in c_segs], -1)
    logits = logits.astype(jnp.float32) * (C_NOPE_DIM + C_ROPE_DIM) ** -0.5
    logits = jnp.where(k_pos[None, :] <= q_pos[:, None], logits, -jnp.inf)
    pr = jax.nn.softmax(logits, axis=-1)
    o_lat = _seg_sum('bhqs,bsc->bqhc', pr, [c[..., :C_KV_RANK] for c in c_segs])
    return jnp.einsum('bqhc,chd->bqhd', o_lat, w_uv)


def _merge(x, p, o_a, o_b, o_c, w_out, g, b):
    bn, t = x.shape[:2]
    mixed = jnp.concatenate([
        o_a.reshape(bn, t, A_WIDTH) * jax.nn.silu(p["a_g"]),
        o_b.reshape(bn, t, B_WIDTH) * jax.nn.silu(p["b_g"]),
        o_c.reshape(bn, t, C_WIDTH) * jax.nn.silu(p["c_g"])], -1)
    return _layernorm(DN_ALPHA * x + mixed @ w_out, g, b)


def _prompt_mixers(p, pos, n_sel, rel_bias, w_uk, w_uv):
    t = pos.shape[0]
    nb = -(-t // Q_BLOCK)
    tp = nb * Q_BLOCK

    def blocks(a):
        a = jnp.pad(a, [(0, 0), (0, tp - t)] + [(0, 0)] * (a.ndim - 2))
        return jnp.moveaxis(a.reshape((a.shape[0], nb, Q_BLOCK) + a.shape[2:]), 1, 0)

    def unblocks(o):
        o = jnp.moveaxis(o, 0, 1)
        return o.reshape((o.shape[0], tp) + o.shape[3:])[:, :t]

    qpos = jnp.arange(tp, dtype=jnp.int32).reshape(nb, Q_BLOCK)
    a_kv, a_ik, b_kv, c_row = p["a_kv"], p["a_ik"], p["b_kv"], p["c_row"]

    def a_block(args):
        qb, iqb, iwb, qp = args
        sel = _indexer_select(iqb, iwb, [a_ik], qp, pos, n_sel)
        kv_sel = jax.vmap(lambda kv, s: kv[s])(a_kv, sel)
        return _sparse_attend(qb, kv_sel, sel, qp, rel_bias)

    o_a = unblocks(lax.map(a_block, (blocks(p["a_q"]), blocks(p["a_iq"]), blocks(p["a_iw"]), qpos)))
    o_b = unblocks(lax.map(lambda args: _stick_breaking(args[0], [b_kv], args[1], pos),
                           (blocks(p["b_q"]), qpos)))
    o_c = unblocks(lax.map(lambda args: _mla_attend(args[0], args[1], [c_row], args[2], pos, w_uk, w_uv),
                           (blocks(p["c_qn"]), blocks(p["c_qr"]), qpos)))
    return o_a, o_b, o_c


def _past(cache, l, page_table):
    g = cache[page_table, l]
    return g.reshape((g.shape[0], g.shape[1] * g.shape[2]) + g.shape[3:])


def _gather_a(cache_a_kv, l, page_table, new_kv, sel, past_len):
    db, s, k = sel.shape
    past_idx = jnp.minimum(sel, past_len - 1)
    page = jnp.take_along_axis(page_table, (past_idx // PAGE_SIZE).reshape(db, s * k), axis=1).reshape(db, s, k)
    kv_past = cache_a_kv[page, l, past_idx % PAGE_SIZE]
    kv_new = jax.vmap(lambda kv, i: kv[i])(new_kv, jnp.clip(sel - past_len, 0, new_kv.shape[1] - 1))
    return jnp.where((sel < past_len)[..., None, None], kv_past, kv_new.astype(kv_past.dtype))


def _sample_mixers(p, l, pos, k_pos, n_sel, page_table, cache_a_kv, cache_a_kidx, cache_b_kv, cache_c_kv,
                   rel_bias, w_uk, w_uv):
    past_len = page_table.shape[1] * PAGE_SIZE
    sel = _indexer_select(p["a_iq"], p["a_iw"], [_past(cache_a_kidx, l, page_table), p["a_ik"]],
                          pos, k_pos, n_sel)
    kv_sel = _gather_a(cache_a_kv, l, page_table, p["a_kv"], sel, past_len)
    o_a = _sparse_attend(p["a_q"], kv_sel, sel, pos, rel_bias)
    o_b = _stick_breaking(p["b_q"], [_past(cache_b_kv, l, page_table), p["b_kv"]], pos, k_pos)
    o_c = _mla_attend(p["c_qn"], p["c_qr"], [_past(cache_c_kv, l, page_table), p["c_row"]],
                      pos, k_pos, w_uk, w_uv)
    return o_a, o_b, o_c


def setup_inputs(seed: int = 0) -> dict:
    key = jax.random.key(seed)
    ks = jax.random.split(key, 24)
    f32 = jnp.float32
    n_pages = PAST_LEN // PAGE_SIZE
    n_used = DEC_BATCH * n_pages
    n_phys = (5 * n_used) // 4
    page_table = jax.random.permutation(ks[0], n_phys)[:n_used].reshape(DEC_BATCH, n_pages).astype(jnp.int32)

    def nrm(k, shape, s=1.0):
        return s * jax.random.normal(k, shape, f32)

    return {
        "x_prompt": nrm(ks[1], (BATCH, SEQ, D_MODEL)),
        "x_sample": nrm(ks[2], (DEC_BATCH, DEC_SEQ, D_MODEL)),
        "cache_a_kv": nrm(ks[3], (n_phys, DEPTH, PAGE_SIZE, 2, HEAD_DIM)),
        "cache_a_kidx": nrm(ks[4], (n_phys, DEPTH, PAGE_SIZE, IDX_DIM)),
        "cache_b_kv": nrm(ks[5], (n_phys, DEPTH, PAGE_SIZE, 2, B_KV_HEADS, HEAD_DIM)),
        "cache_c_kv": nrm(ks[6], (n_phys, DEPTH, PAGE_SIZE, C_KV_RANK + C_ROPE_DIM)),
        "page_table": page_table,
        "meta_tokens": nrm(ks[7], (N_META, D_MODEL)),
        "rel_bias": nrm(ks[8], (N_BUCKETS, A_HEADS), 0.5),
        "emb_ln_g": 1.0 + nrm(ks[9], (D_MODEL,), 0.01),
        "emb_ln_b": nrm(ks[10], (D_MODEL,), 0.01),
        "w_in": nrm(ks[11], (DEPTH, D_MODEL, IN_WIDTH), D_MODEL ** -0.5),
        "c_q_norm": 1.0 + nrm(ks[12], (DEPTH, C_Q_RANK), 0.01),
        "c_kv_norm": 1.0 + nrm(ks[13], (DEPTH, C_KV_RANK), 0.01),
        "w_uq": nrm(ks[14], (DEPTH, C_Q_RANK, C_HEADS * (C_NOPE_DIM + C_ROPE_DIM)), C_Q_RANK ** -0.5),
        "w_uk": nrm(ks[15], (DEPTH, C_KV_RANK, C_HEADS, C_NOPE_DIM), C_KV_RANK ** -0.5),
        "w_uv": nrm(ks[16], (DEPTH, C_KV_RANK, C_HEADS, C_V_DIM), C_KV_RANK ** -0.5),
        "w_out": nrm(ks[17], (DEPTH, MIX_WIDTH, D_MODEL), DN_BETA * MIX_WIDTH ** -0.5),
        "ln_g": 1.0 + nrm(ks[18], (DEPTH, D_MODEL), 0.01),
        "ln_b": nrm(ks[19], (DEPTH, D_MODEL), 0.01),
    }


def reference(x_prompt, x_sample, cache_a_kv, cache_a_kidx, cache_b_kv, cache_c_kv, page_table,
              meta_tokens, rel_bias, emb_ln_g, emb_ln_b, w_in, c_q_norm, c_kv_norm, w_uq, w_uk, w_uv,
              w_out, ln_g, ln_b):
    bp = x_prompt.shape[0]
    meta = jnp.broadcast_to(meta_tokens[None].astype(x_prompt.dtype), (bp, N_META, D_MODEL))
    hp = _layernorm(jnp.concatenate([meta, x_prompt], axis=1), emb_ln_g, emb_ln_b)
    hs = _layernorm(x_sample, emb_ln_g, emb_ln_b)
    t_p, s_dec = hp.shape[1], hs.shape[1]
    past_len = page_table.shape[1] * PAGE_SIZE
    pos_p = jnp.arange(t_p, dtype=jnp.int32)
    pos_s = past_len + jnp.arange(s_dec, dtype=jnp.int32)
    k_pos_s = jnp.arange(past_len + s_dec, dtype=jnp.int32)
    n_sel_p = min(TOPK_MAX, SEQ // 4)
    n_sel_s = min(TOPK_MAX, (PAST_LEN + DEC_SEQ) // 4)
    rows_p, rows_s = [], []
    for l in range(DEPTH):
        pp = _project(hp, pos_p, w_in[l], c_q_norm[l], c_kv_norm[l], w_uq[l])
        oa, ob, oc = _prompt_mixers(pp, pos_p, n_sel_p, rel_bias, w_uk[l], w_uv[l])
        hp = _merge(hp, pp, oa, ob, oc, w_out[l], ln_g[l], ln_b[l])
        rows_p.append((pp["a_kv"], pp["a_ik"], pp["b_kv"], pp["c_row"]))

        ps = _project(hs, pos_s, w_in[l], c_q_norm[l], c_kv_norm[l], w_uq[l])
        oa, ob, oc = _sample_mixers(ps, l, pos_s, k_pos_s, n_sel_s, page_table, cache_a_kv, cache_a_kidx,
                                    cache_b_kv, cache_c_kv, rel_bias, w_uk[l], w_uv[l])
        hs = _merge(hs, ps, oa, ob, oc, w_out[l], ln_g[l], ln_b[l])
        rows_s.append((ps["a_kv"], ps["a_ik"], ps["b_kv"], ps["c_row"]))

    y_prompt = hp[:, N_META:]
    y_sample = hs
    new_a_kv_p = jnp.stack([r[0] for r in rows_p], axis=1)
    new_a_kidx_p = jnp.stack([r[1] for r in rows_p], axis=1)
    new_b_kv_p = jnp.stack([r[2] for r in rows_p], axis=1)
    new_c_kv_p = jnp.stack([r[3] for r in rows_p], axis=1)
    new_a_kv_s = jnp.stack([r[0] for r in rows_s], axis=1)
    new_a_kidx_s = jnp.stack([r[1] for r in rows_s], axis=1)
    new_b_kv_s = jnp.stack([r[2] for r in rows_s], axis=1)
    new_c_kv_s = jnp.stack([r[3] for r in rows_s], axis=1)
    return (y_prompt, y_sample, new_a_kv_p, new_a_kidx_p, new_b_kv_p, new_c_kv_p,
            new_a_kv_s, new_a_kidx_s, new_b_kv_s, new_c_kv_s)
```

```python
import functools
import math

import jax
import jax.numpy as jnp
from jax import lax
from jax.experimental import pallas as pl
from jax.experimental.pallas import tpu as pltpu

F32, BF16, I32 = jnp.float32, jnp.bfloat16, jnp.int32

N_META = 16
HEAD_DIM = 64
A_HEADS = 4
IDX_HEADS = 8
IDX_DIM = 64
TOPK_MAX = 256
B_HEADS = 4
B_KV_HEADS = 2
C_HEADS = 8
C_NOPE_DIM = 64
C_ROPE_DIM = 32
C_KV_RANK = 128
N_BUCKETS = 32
MAX_DISTANCE = 128
ROPE_THETA = 10000.0
PAGE_SIZE = 128
LN_EPS = 1e-5
RMS_EPS = 1e-6

LANE = 128
TILE = 128
VMEM_LIMIT = 56 * 1024 * 1024
NEG = -1e30
INT_MIN = -2 ** 31
C_SCALE = (C_NOPE_DIM + C_ROPE_DIM) ** -0.5
IW_SCALE = IDX_HEADS ** -0.5 * IDX_DIM ** -0.5

SEG_AQ = (0, 256)
SEG_AKV = (256, 384)
SEG_AIQ = (384, 896)
SEG_AIK = (896, 1024)
SEG_G = (1024, 2048)
SEG_BQ = (2048, 2304)
SEG_BKV = (2304, 2560)
SEG_CQ = (2560, 2816)
SEG_CKV = (2816, 2944)
SEG_CKR = (2944, 3072)
IN_PERM_WIDTH = 3072


def _params(*sem):
    return pltpu.CompilerParams(dimension_semantics=sem, vmem_limit_bytes=VMEM_LIMIT)


def _nt_dot(a, b):
    return lax.dot_general(a, b, (((1,), (1,)), ((), ())), preferred_element_type=F32)


def _dot(a, b):
    return jnp.dot(a, b, preferred_element_type=F32)


def _ln_kernel(x_ref, g_ref, b_ref, o_ref):
    x = x_ref[...]
    mu = jnp.mean(x, -1, keepdims=True)
    xc = x - mu
    var = jnp.mean(xc * xc, -1, keepdims=True)
    o_ref[...] = xc * lax.rsqrt(var + LN_EPS) * g_ref[...] + b_ref[...]


def _layernorm(x2d, g, b, rows):
    n, d = x2d.shape
    return pl.pallas_call(
        _ln_kernel,
        out_shape=jax.ShapeDtypeStruct((n, d), F32),
        grid=(n // rows,),
        in_specs=[pl.BlockSpec((rows, d), lambda i: (i, 0)),
                  pl.BlockSpec((1, d), lambda i: (0, 0)),
                  pl.BlockSpec((1, d), lambda i: (0, 0))],
        out_specs=pl.BlockSpec((rows, d), lambda i: (i, 0)),
        compiler_params=_params("arbitrary"),
        name="embed_ln",
    )(x2d, g.reshape(1, d), b.reshape(1, d))


def _project_kernel(x_ref, w_ref, qn_ref, kvn_ref, wuq_ref, wukt_ref, sel_ref,
                    cq_ref, sq_ref, ck_ref, sa_ref, sb_ref,
                    aq_ref, akv32_ref, akv16_ref, aiq_ref, aik32_ref, aik16_ref, aiw_ref, g_ref,
                    bq_ref, bkv32_ref, bkvh_ref, cqf_ref, crow32_ref, crow16_ref):
    xb = x_ref[0].astype(BF16)

    def seg(ab):
        return _dot(xb, w_ref[:, ab[0]:ab[1]])

    s = seg(SEG_AQ)
    for h in range(A_HEADS):
        aq_ref[0, h] = (s[:, h * 64:(h + 1) * 64] * HEAD_DIM ** -0.5).astype(BF16)
    s = seg(SEG_AKV)
    akv32_ref[0] = s
    akv16_ref[0] = s.astype(BF16)
    s = seg(SEG_AIQ)
    for h in range(IDX_HEADS):
        aiq_ref[0, h] = s[:, h * 64:(h + 1) * 64].astype(BF16)
    s = seg(SEG_AIK)
    aik32_ref[0] = s[:, :IDX_DIM]
    aik16_ref[0] = s.astype(BF16)
    aiw_ref[0] = s * IW_SCALE
    g_ref[0] = seg(SEG_G)
    s = seg(SEG_BQ)
    for h in range(B_HEADS):
        bq_ref[0, h] = (s[:, h * 64:(h + 1) * 64] * HEAD_DIM ** -0.5).astype(BF16)
    s = seg(SEG_BKV)
    bkv32_ref[0] = s
    for h in range(2 * B_KV_HEADS):
        bkvh_ref[0, h] = s[:, h * 64:(h + 1) * 64].astype(BF16)

    cq = seg(SEG_CQ)
    cqn = cq * lax.rsqrt(jnp.mean(cq * cq, -1, keepdims=True) + RMS_EPS) * qn_ref[...]
    cu = _dot(cqn.astype(BF16), wuq_ref[...])
    r1, r2 = cu[:, 512:640], cu[:, 640:768]
    c, sn = cq_ref[...], sq_ref[...]
    rot = (jnp.concatenate([r1 * c - r2 * sn, r1 * sn + r2 * c], axis=1) * C_SCALE).astype(BF16)
    for h in range(C_HEADS):
        qlat = _dot(cu[:, h * 64:(h + 1) * 64].astype(BF16), wukt_ref[h]) * C_SCALE
        qrope = _dot(rot, sel_ref[h])
        cqf_ref[0, h] = jnp.concatenate([qlat, qrope], axis=1).astype(BF16)

    ckv = seg(SEG_CKV)
    ckvn = ckv * lax.rsqrt(jnp.mean(ckv * ckv, -1, keepdims=True) + RMS_EPS) * kvn_ref[...]
    kr = seg(SEG_CKR)
    krot = kr * ck_ref[...] + pltpu.roll(kr, LANE - 16, 1) * sa_ref[...] + pltpu.roll(kr, 16, 1) * sb_ref[...]
    crow32_ref[0, :, 0:C_KV_RANK] = ckvn
    crow32_ref[0, :, C_KV_RANK:C_KV_RANK + C_ROPE_DIM] = krot[:, :C_ROPE_DIM]
    crow16_ref[0] = jnp.concatenate([ckvn, krot], axis=1).astype(BF16)


def _project(x, t_out, rows, wts, tabs):
    bn, t, d = x.shape
    nt = t // rows
    w_in, qn, kvn, wuq, wukt, sel = wts

    def full(a):
        nd = a.ndim
        return pl.BlockSpec(a.shape, lambda b, i, _n=nd: (0,) * _n)

    def rows3(w):
        return pl.BlockSpec((1, rows, w), lambda b, i: (b, i, 0))

    def heads4(nh, w):
        return pl.BlockSpec((1, nh, rows, w), lambda b, i: (b, 0, i, 0))

    tab_spec = pl.BlockSpec((rows, LANE), lambda b, i: (i, 0))
    out_shape = [
        jax.ShapeDtypeStruct((bn, A_HEADS, t, 64), BF16),
        jax.ShapeDtypeStruct((bn, t_out, 128), F32),
        jax.ShapeDtypeStruct((bn, t, 128), BF16),
        jax.ShapeDtypeStruct((bn, IDX_HEADS, t, 64), BF16),
        jax.ShapeDtypeStruct((bn, t_out, 64), F32),
        jax.ShapeDtypeStruct((bn, t, 128), BF16),
        jax.ShapeDtypeStruct((bn, t, 128), F32),
        jax.ShapeDtypeStruct((bn, t, 1024), F32),
        jax.ShapeDtypeStruct((bn, B_HEADS, t, 64), BF16),
        jax.ShapeDtypeStruct((bn, t_out, 256), F32),
        jax.ShapeDtypeStruct((bn, 4, t, 64), BF16),
        jax.ShapeDtypeStruct((bn, C_HEADS, t, 256), BF16),
        jax.ShapeDtypeStruct((bn, t_out, 160), F32),
        jax.ShapeDtypeStruct((bn, t, 256), BF16),
    ]
    out_specs = [heads4(A_HEADS, 64), rows3(128), rows3(128), heads4(IDX_HEADS, 64), rows3(64), rows3(128),
                 rows3(128), rows3(1024), heads4(B_HEADS, 64), rows3(256), heads4(4, 64),
                 heads4(C_HEADS, 256), rows3(160), rows3(256)]
    return pl.pallas_call(
        _project_kernel,
        out_shape=out_shape,
        grid=(bn, nt),
        in_specs=[rows3(d), full(w_in), full(qn), full(kvn), full(wuq), full(wukt), full(sel)] + [tab_spec] * 5,
        out_specs=out_specs,
        compiler_params=_params("arbitrary", "arbitrary"),
        name="project",
    )(x, w_in, qn, kvn, wuq, wukt, sel, *tabs)


def _merge_kernel(x_ref, g_ref, oa_ref, ob_ref, oc_ref, w_ref, lg_ref, lb_ref, y_ref, *, alpha):
    g = g_ref[0]
    o = jnp.concatenate([oa_ref[0], ob_ref[0], oc_ref[0]], axis=1)
    mixed = (o * (g * jax.nn.sigmoid(g))).astype(BF16)
    z = alpha * x_ref[0] + _dot(mixed, w_ref[...])
    mu = jnp.mean(z, -1, keepdims=True)
    zc = z - mu
    var = jnp.mean(zc * zc, -1, keepdims=True)
    y_ref[0] = zc * lax.rsqrt(var + LN_EPS) * lg_ref[...] + lb_ref[...]


def _merge(x, gates, oa, ob, oc, w_out, lg, lb, rows, alpha):
    bn, t, d = x.shape

    def rows3(w):
        return pl.BlockSpec((1, rows, w), lambda b, i: (b, i, 0))

    def full(a):
        nd = a.ndim
        return pl.BlockSpec(a.shape, lambda b, i, _n=nd: (0,) * _n)

    lg2, lb2 = lg.reshape(1, d), lb.reshape(1, d)
    return pl.pallas_call(
        functools.partial(_merge_kernel, alpha=alpha),
        out_shape=jax.ShapeDtypeStruct((bn, t, d), F32),
        grid=(bn, t // rows),
        in_specs=[rows3(d), rows3(1024), rows3(256), rows3(256), rows3(512), full(w_out), full(lg2), full(lb2)],
        out_specs=rows3(d),
        compiler_params=_params("arbitrary", "arbitrary"),
        name="merge",
    )(x, gates, oa, ob, oc, w_out, lg2, lb2)


def _t5_bias_tile(rb_ref, h, dist):
    max_exact = N_BUCKETS // 2
    lg = jnp.log(jnp.maximum(dist, 1).astype(F32) / max_exact) / math.log(MAX_DISTANCE / max_exact)
    large = jnp.minimum(max_exact + (lg * (N_BUCKETS - max_exact)).astype(I32), N_BUCKETS - 1)
    bucket = jnp.where(dist < max_exact, dist, large)
    out = jnp.zeros(dist.shape, F32)
    for k in range(N_BUCKETS):
        out = jnp.where(bucket == k, rb_ref[k, h], out)
    return out


def _sortable(score):
    bits = pltpu.bitcast(score + 0.0, I32)
    return bits ^ ((bits >> 31) & 0x7FFFFFFF)


def _kth_largest(count_ge, shape, k):
    def body(t, tau):
        bit = lax.shift_left(jnp.int32(1), 31 - t)
        cand = tau + bit
        return jnp.where(count_ge(cand) >= k, cand, tau)

    return lax.fori_loop(0, 32, body, jnp.full(shape, INT_MIN, I32))


def _attn_a_kernel(rb_ref, iq_ref, iw_ref, q_ref, ik_ref, kv_ref, pfx_ref, o_ref,
                   bias_scr, wb_scr, key_scr, msk_scr, lg_scr, mx_scr, acc_scr, ls_scr, *, n_sel):
    b, i = pl.program_id(0), pl.program_id(1)
    nkt = i + 1
    row = lax.broadcasted_iota(I32, (TILE, TILE), 0)
    col = lax.broadcasted_iota(I32, (TILE, TILE), 1)

    @pl.when((b == 0) & (i == 0))
    def _():
        for h in range(A_HEADS):
            bias_scr[h, 0] = _t5_bias_tile(rb_ref, h, jnp.maximum(row - col, 0))
            bias_scr[h, 1] = _t5_bias_tile(rb_ref, h, TILE + row - col)
            bias_scr[h, 2] = _t5_bias_tile(rb_ref, h, jnp.full((TILE, TILE), 2 * TILE, I32))

    iw = iw_ref[0]
    for h in range(IDX_HEADS):
        wb_scr[h] = jnp.broadcast_to(iw[:, IDX_DIM + h:IDX_DIM + h + 1], (TILE, TILE))
    iqs = iq_ref[0].reshape(IDX_HEADS * TILE, IDX_DIM)

    def score_tile(j, carry):
        kt = ik_ref[0, pl.ds(pl.multiple_of(j * TILE, TILE), TILE), :][:, :IDX_DIM]
        s = _nt_dot(iqs, kt)
        acc = jnp.zeros((TILE, TILE), F32)
        for h in range(IDX_HEADS):
            acc = acc + wb_scr[h] * jnp.maximum(s[h * TILE:(h + 1) * TILE], 0.0)
        key_scr[j] = _sortable(acc)
        return carry

    lax.fori_loop(0, nkt, score_tile, 0)
    key_scr[i] = jnp.where(col <= row, key_scr[i], INT_MIN)

    def count_ge(cand):
        cb = jnp.broadcast_to(cand, (TILE, TILE))
        c = lax.fori_loop(0, nkt, lambda j, c: c + jnp.where(key_scr[j] >= cb, 1, 0),
                          jnp.zeros((TILE, TILE), I32))
        return jnp.sum(c, axis=1, keepdims=True)

    tau = _kth_largest(count_ge, (TILE, 1), n_sel)
    tau_b = jnp.broadcast_to(jnp.maximum(tau, INT_MIN + 1), (TILE, TILE))
    n_gt = jnp.sum(lax.fori_loop(0, nkt, lambda j, c: c + jnp.where(key_scr[j] > tau_b, 1, 0),
                                 jnp.zeros((TILE, TILE), I32)), axis=1, keepdims=True)
    take_b = jnp.broadcast_to((n_sel - n_gt).astype(F32), (TILE, TILE))

    def mask_tile(j, seen):
        kt = key_scr[j]
        eq = kt == tau_b
        both = _dot(jnp.where(eq, 1.0, 0.0).astype(BF16), pfx_ref[...])
        sel = (kt > tau_b) | (eq & (both[:, :TILE] + seen < take_b))
        msk_scr[j] = jnp.where(sel, 0.0, NEG)
        return seen + both[:, TILE:]

    lax.fori_loop(0, nkt, mask_tile, jnp.zeros((TILE, TILE), F32))

    qs = q_ref[0].reshape(A_HEADS * TILE, HEAD_DIM)
    mx_scr[...] = jnp.full(mx_scr.shape, NEG, F32)

    def logits_tile(j, carry):
        kvt = kv_ref[0, pl.ds(pl.multiple_of(j * TILE, TILE), TILE), :]
        lg = _nt_dot(qs, kvt[:, :HEAD_DIM])
        bidx = jnp.minimum(i - j, 2)
        madd = msk_scr[j]
        for h in range(A_HEADS):
            sl = slice(h * TILE, (h + 1) * TILE)
            lgh = lg[sl] + bias_scr[h, bidx] + madd
            lg_scr[j, sl] = lgh
            mx_scr[sl] = jnp.maximum(mx_scr[sl], lgh)
        return carry

    lax.fori_loop(0, nkt, logits_tile, 0)
    m_b = jnp.broadcast_to(jnp.max(mx_scr[...], axis=1, keepdims=True), mx_scr.shape)
    acc_scr[...] = jnp.zeros(acc_scr.shape, F32)
    ls_scr[...] = jnp.zeros(ls_scr.shape, F32)

    def pv_tile(j, carry):
        kvt = kv_ref[0, pl.ds(pl.multiple_of(j * TILE, TILE), TILE), :]
        p = jnp.exp(lg_scr[j] - m_b)
        ls_scr[...] += p
        acc_scr[...] += _dot(p.astype(BF16), kvt)
        return carry

    lax.fori_loop(0, nkt, pv_tile, 0)
    o = acc_scr[...] / jnp.sum(ls_scr[...], axis=1, keepdims=True)
    o_ref[0] = jnp.concatenate([o[h * TILE:(h + 1) * TILE, HEAD_DIM:] for h in range(A_HEADS)], axis=1)


def _attn_a(rel_bias, aiq, aiw, aq, aik16, akv16, pfx, n_sel):
    bn, _, t, _ = aq.shape
    nt = t // TILE
    return pl.pallas_call(
        functools.partial(_attn_a_kernel, n_sel=n_sel),
        out_shape=jax.ShapeDtypeStruct((bn, t, A_HEADS * HEAD_DIM), F32),
        grid=(bn, nt),
        in_specs=[pl.BlockSpec(memory_space=pltpu.SMEM),
                  pl.BlockSpec((1, IDX_HEADS, TILE, 64), lambda b, i: (b, 0, i, 0)),
                  pl.BlockSpec((1, TILE, 128), lambda b, i: (b, i, 0)),
                  pl.BlockSpec((1, A_HEADS, TILE, 64), lambda b, i: (b, 0, i, 0)),
                  pl.BlockSpec((1, t, 128), lambda b, i: (b, 0, 0)),
                  pl.BlockSpec((1, t, 128), lambda b, i: (b, 0, 0)),
                  pl.BlockSpec((TILE, 2 * TILE), lambda b, i: (0, 0))],
        out_specs=pl.BlockSpec((1, TILE, A_HEADS * HEAD_DIM), lambda b, i: (b, i, 0)),
        scratch_shapes=[pltpu.VMEM((A_HEADS, 3, TILE, TILE), F32),
                        pltpu.VMEM((IDX_HEADS, TILE, TILE), F32),
                        pltpu.VMEM((nt, TILE, TILE), I32),
                        pltpu.VMEM((nt, TILE, TILE), F32),
                        pltpu.VMEM((nt, A_HEADS * TILE, TILE), F32),
                        pltpu.VMEM((A_HEADS * TILE, TILE), F32),
                        pltpu.VMEM((A_HEADS * TILE, TILE), F32),
                        pltpu.VMEM((A_HEADS * TILE, TILE), F32)],
        compiler_params=_params("arbitrary", "arbitrary"),
        name="attn_a_prompt",
    )(rel_bias, aiq, aiw, aq, aik16, akv16, pfx)


def _log_sigmoid(z):
    return jnp.minimum(z, 0.0) - jnp.log1p(jnp.exp(-jnp.abs(z)))


def _attn_b_kernel(q_ref, kv_ref, sfx_ref, o_ref):
    i = pl.program_id(1)
    rows = (B_HEADS // B_KV_HEADS) * TILE
    row = lax.broadcasted_iota(I32, (rows, TILE), 0) & (TILE - 1)
    col = lax.broadcasted_iota(I32, (rows, TILE), 1)
    outs = []
    for g in range(B_KV_HEADS):
        qg = q_ref[0, 2 * g:2 * g + 2].reshape(rows, HEAD_DIM)

        def body(jj, carry, g=g, qg=qg):
            right, acc = carry
            j = i - jj
            ks = pl.ds(pl.multiple_of(j * TILE, TILE), TILE)
            z = _nt_dot(qg, kv_ref[0, g, ks, :])
            ls = _log_sigmoid(z)
            m = (col < row) | (j < i)
            lneg = jnp.where(m, ls - z, 0.0)
            hi = lneg.astype(BF16)
            lo = (lneg - hi.astype(F32)).astype(BF16)
            cum = _dot(hi, sfx_ref[...]) + _dot(lo, sfx_ref[...])
            w = jnp.where(m, jnp.exp(ls + cum[:, :TILE] + right), 0.0)
            acc = acc + _dot(w.astype(BF16), kv_ref[0, B_KV_HEADS + g, ks, :])
            return right + cum[:, TILE:], acc

        _, acc = lax.fori_loop(0, i + 1, body, (jnp.zeros((rows, TILE), F32), jnp.zeros((rows, HEAD_DIM), F32)))
        outs += [acc[:TILE], acc[TILE:]]
    o_ref[0] = jnp.concatenate(outs, axis=1)


def _attn_b(bq, bkvh, sfx):
    bn, _, t, _ = bq.shape
    return pl.pallas_call(
        _attn_b_kernel,
        out_shape=jax.ShapeDtypeStruct((bn, t, B_HEADS * HEAD_DIM), F32),
        grid=(bn, t // TILE),
        in_specs=[pl.BlockSpec((1, B_HEADS, TILE, 64), lambda b, i: (b, 0, i, 0)),
                  pl.BlockSpec((1, 4, t, 64), lambda b, i: (b, 0, 0, 0)),
                  pl.BlockSpec((TILE, 2 * TILE), lambda b, i: (0, 0))],
        out_specs=pl.BlockSpec((1, TILE, B_HEADS * HEAD_DIM), lambda b, i: (b, i, 0)),
        compiler_params=_params("arbitrary", "arbitrary"),
        name="attn_b_prompt",
    )(bq, bkvh, sfx)


def _attn_c_kernel(q_ref, c_ref, wuv_ref, o_ref, lg_scr, mx_scr, acc_scr, ls_scr):
    i = pl.program_id(1)
    rows = C_HEADS * TILE
    row = lax.broadcasted_iota(I32, (rows, TILE), 0) & (TILE - 1)
    col = lax.broadcasted_iota(I32, (rows, TILE), 1)
    qs = q_ref[0].reshape(rows, 2 * LANE)
    mx_scr[...] = jnp.full(mx_scr.shape, NEG, F32)

    def logits_tile(j, carry):
        ct = c_ref[0, pl.ds(pl.multiple_of(j * TILE, TILE), TILE), :]
        lg = jnp.where((col <= row) | (j < i), _nt_dot(qs, ct), NEG)
        lg_scr[j] = lg
        mx_scr[...] = jnp.maximum(mx_scr[...], lg)
        return carry

    lax.fori_loop(0, i + 1, logits_tile, 0)
    m_b = jnp.broadcast_to(jnp.max(mx_scr[...], axis=1, keepdims=True), (rows, TILE))
    acc_scr[...] = jnp.zeros(acc_scr.shape, F32)
    ls_scr[...] = jnp.zeros(ls_scr.shape, F32)

    def pv_tile(j, carry):
        ct = c_ref[0, pl.ds(pl.multiple_of(j * TILE, TILE), TILE), :]
        p = jnp.exp(lg_scr[j] - m_b)
        ls_scr[...] += p
        acc_scr[...] += _dot(p.astype(BF16), ct[:, :C_KV_RANK])
        return carry

    lax.fori_loop(0, i + 1, pv_tile, 0)
    ol = (acc_scr[...] / jnp.sum(ls_scr[...], axis=1, keepdims=True)).astype(BF16)
    o_ref[0] = jnp.concatenate([_dot(ol[h * TILE:(h + 1) * TILE], wuv_ref[h]) for h in range(C_HEADS)], axis=1)


def _attn_c(cqf, crow16, wuv):
    bn, _, t, _ = cqf.shape
    nt = t // TILE
    rows = C_HEADS * TILE
    return pl.pallas_call(
        _attn_c_kernel,
        out_shape=jax.ShapeDtypeStruct((bn, t, C_HEADS * HEAD_DIM), F32),
        grid=(bn, nt),
        in_specs=[pl.BlockSpec((1, C_HEADS, TILE, 256), lambda b, i: (b, 0, i, 0)),
                  pl.BlockSpec((1, t, 256), lambda b, i: (b, 0, 0)),
                  pl.BlockSpec(wuv.shape, lambda b, i: (0, 0, 0))],
        out_specs=pl.BlockSpec((1, TILE, C_HEADS * HEAD_DIM), lambda b, i: (b, i, 0)),
        scratch_shapes=[pltpu.VMEM((nt, rows, TILE), F32),
                        pltpu.VMEM((rows, TILE), F32),
                        pltpu.VMEM((rows, TILE), F32),
                        pltpu.VMEM((rows, TILE), F32)],
        compiler_params=_params("arbitrary", "arbitrary"),
        name="attn_c_prompt",
    )(cqf, crow16, wuv)


def _permute_w_in(w):
    d = w.shape[0]
    names = dict(a_q=(0, 256), a_k=(256, 320), a_v=(320, 384), a_iq=(384, 896), a_ik=(896, 960), a_iw=(960, 968),
                 a_g=(968, 1224), b_q=(1224, 1480), b_k=(1480, 1608), b_v=(1608, 1736), b_g=(1736, 1992),
                 c_q=(1992, 2248), c_kv=(2248, 2376), c_kr=(2376, 2408), c_g=(2408, 2920))

    def c(n):
        return w[:, names[n][0]:names[n][1]]

    z = lambda n: jnp.zeros((d, n), w.dtype)
    out = jnp.concatenate([c("a_q"), c("a_k"), c("a_v"), c("a_iq"), c("a_ik"), c("a_iw"), z(56),
                           c("a_g"), c("b_g"), c("c_g"), c("b_q"), c("b_k"), c("b_v"),
                           c("c_q"), c("c_kv"), c("c_kr"), z(96)], axis=1)
    assert out.shape[1] == IN_PERM_WIDTH
    return out.astype(BF16)


def _permute_w_uq(w):
    r = w.reshape(w.shape[0], C_HEADS, C_NOPE_DIM + C_ROPE_DIM)
    half = C_ROPE_DIM // 2
    nope = r[:, :, :C_NOPE_DIM].reshape(w.shape[0], -1)
    lo = r[:, :, C_NOPE_DIM:C_NOPE_DIM + half].reshape(w.shape[0], -1)
    hi = r[:, :, C_NOPE_DIM + half:].reshape(w.shape[0], -1)
    return jnp.concatenate([nope, lo, hi], axis=1).astype(BF16)


def _rope_select():
    half = C_ROPE_DIM // 2
    src = jnp.arange(2 * LANE)[:, None]
    dst = jnp.arange(LANE)[None, :]
    mats = []
    for h in range(C_HEADS):
        lo = (src == h * half + dst) & (dst < half)
        hi = (src == LANE + h * half + dst - half) & (dst >= half) & (dst < 2 * half)
        mats.append(lo | hi)
    return jnp.stack(mats).astype(BF16)


def _rope_tables(pos):
    half = C_ROPE_DIM // 2
    freq = ROPE_THETA ** (-jnp.arange(half, dtype=F32) / half)
    ang = pos.astype(F32)[:, None] * freq
    c, s = jnp.cos(ang), jnp.sin(ang)
    cq, sq = jnp.tile(c, (1, LANE // half)), jnp.tile(s, (1, LANE // half))
    z = jnp.zeros((pos.shape[0], LANE - 2 * half), F32)
    zh = jnp.zeros_like(c)
    ck = jnp.concatenate([c, c, z], axis=1)
    sa = jnp.concatenate([-s, zh, z], axis=1)
    sb = jnp.concatenate([zh, s, z], axis=1)
    return cq, sq, ck, sa, sb


def _prefix_matrices():
    a = jnp.arange(TILE)
    ones = jnp.ones((TILE, TILE), F32)
    pfx = jnp.concatenate([(a[:, None] < a[None, :]).astype(F32), ones], axis=1).astype(BF16)
    sfx = jnp.concatenate([(a[:, None] > a[None, :]).astype(F32), ones], axis=1).astype(BF16)
    return pfx, sfx


def _paged_loop(pt_ref, b, cache_ref, layer, buf, sem, n_chunks, ch, reverse, compute, init):
    def chunk_of(step):
        return n_chunks - 1 - step if reverse else step

    def copies(page_of, slot):
        return [pltpu.make_async_copy(cache_ref.at[page_of(k), layer], buf.at[slot, k], sem.at[slot])
                for k in range(ch)]

    def start(step, slot):
        c = chunk_of(step)
        for cp in copies(lambda k: pt_ref[b, c * ch + k], slot):
            cp.start()

    def wait(slot):
        for cp in copies(lambda k: 0, slot):
            cp.wait()

    start(0, 0)

    def body(step, carry):
        slot = step & 1

        @pl.when(step + 1 < n_chunks)
        def _():
            start(step + 1, 1 - slot)

        wait(slot)
        return compute(chunk_of(step), buf.at[slot], carry)

    return lax.fori_loop(0, n_chunks, body, init)


def _lane_tiles(x, n):
    return [x[:, k * LANE:(k + 1) * LANE] for k in range(n)]


def _paged_call(kernel, name, page_table, layer_cache, operands, operand_specs, out_shape, out_spec, ch, scratch=()):
    db = page_table.shape[0]
    width = layer_cache.shape[-1]
    grid_spec = pltpu.PrefetchScalarGridSpec(
        num_scalar_prefetch=1,
        grid=(db,),
        in_specs=list(operand_specs) + [pl.BlockSpec(memory_space=pl.ANY)],
        out_specs=out_spec,
        scratch_shapes=[pltpu.VMEM((2, ch, PAGE_SIZE, width), F32), pltpu.SemaphoreType.DMA((2,))] + list(scratch),
    )
    return pl.pallas_call(kernel, out_shape=out_shape, grid_spec=grid_spec,
                          compiler_params=_params("arbitrary"), name=name)(page_table, *operands, layer_cache)


def _sample_idx_kernel(pt_ref, iq_ref, w_ref, ikn_ref, cache_ref, out_ref, buf, sem, *, layer, n_pages, ch):
    b = pl.program_id(0)
    iq = iq_ref[0]
    w = w_ref[0]

    def score(keys):
        s = jnp.maximum(_nt_dot(iq, keys), 0.0) * w
        return jnp.sum(s, axis=0, keepdims=True)

    def compute(c, view, carry):
        keys = view[...].reshape(ch * PAGE_SIZE, IDX_DIM).astype(BF16)
        sc = score(keys)
        for k, tile in enumerate(_lane_tiles(sc, ch)):
            out_ref[c * ch + k, 0] = tile
        return carry

    _paged_loop(pt_ref, b, cache_ref, layer, buf, sem, n_pages // ch, ch, False, compute, 0)
    s_new = jnp.sum(iq.astype(F32) * ikn_ref[0].astype(F32), axis=1, keepdims=True)
    s_new = jnp.sum(jnp.maximum(s_new, 0.0) * w, axis=0, keepdims=True)
    lane = lax.broadcasted_iota(I32, (1, LANE), 1)
    out_ref[n_pages, 0] = jnp.where(lane == 0, s_new, -jnp.inf)


def _sample_idx(page_table, cache, layer, iq, w, ik_new, ch):
    db, n_pages = page_table.shape
    per_b = lambda shape: pl.BlockSpec((1,) + shape, lambda b, pt: (b, 0, 0))
    out = _paged_call(
        functools.partial(_sample_idx_kernel, layer=layer, n_pages=n_pages, ch=ch), "sample_idx",
        page_table, cache, (iq, w, ik_new), [per_b((IDX_HEADS, IDX_DIM)), per_b((IDX_HEADS, 1)), per_b((1, IDX_DIM))],
        jax.ShapeDtypeStruct((n_pages + 1, db, 1, LANE), F32),
        pl.BlockSpec((n_pages + 1, 1, 1, LANE), lambda b, pt: (0, b, 0, 0)), ch)
    return out.reshape(n_pages + 1, db, LANE)


def _sample_topk_kernel(s_ref, pfx_ref, m_ref, key_scr, *, n_sel):
    nt, bb, _ = s_ref.shape

    def to_key(j, carry):
        key_scr[j] = _sortable(s_ref[j])
        return carry

    lax.fori_loop(0, nt, to_key, 0)

    def count(pred):
        c = lax.fori_loop(0, nt, lambda j, c: c + jnp.where(pred(key_scr[j]), 1, 0), jnp.zeros((bb, LANE), I32))
        return jnp.sum(c, axis=1, keepdims=True)

    def count_ge(cand):
        cb = jnp.broadcast_to(cand, (bb, LANE))
        return count(lambda kt: kt >= cb)

    tau = _kth_largest(count_ge, (bb, 1), n_sel)
    tau_b = jnp.broadcast_to(jnp.maximum(tau, INT_MIN + 1), (bb, LANE))
    take_b = jnp.broadcast_to((n_sel - count(lambda kt: kt > tau_b)).astype(F32), (bb, LANE))

    def mask_tile(j, seen):
        kt = key_scr[j]
        eq = kt == tau_b
        both = _dot(jnp.where(eq, 1.0, 0.0).astype(BF16), pfx_ref[...])
        sel = (kt > tau_b) | (eq & (both[:, :LANE] + seen < take_b))
        m_ref[j] = jnp.where(sel, 0.0, NEG)
        return seen + both[:, LANE:]

    lax.fori_loop(0, nt, mask_tile, jnp.zeros((bb, LANE), F32))


def _sample_topk(scores, pfx, n_sel):
    nt, db, _ = scores.shape
    bb = min(db, 32)
    spec = pl.BlockSpec((nt, bb, LANE), lambda i: (0, i, 0))
    return pl.pallas_call(
        functools.partial(_sample_topk_kernel, n_sel=n_sel),
        out_shape=jax.ShapeDtypeStruct((nt, db, LANE), F32),
        grid=(db // bb,),
        in_specs=[spec, pl.BlockSpec(pfx.shape, lambda i: (0, 0))],
        out_specs=spec,
        scratch_shapes=[pltpu.VMEM((nt, bb, LANE), I32)],
        compiler_params=_params("arbitrary"),
        name="sample_topk",
    )(scores, pfx)


def _softmax_step(lg, values, state):
    m, l, acc = state
    m_new = jnp.maximum(m, jnp.max(lg, axis=1, keepdims=True))
    a = jnp.exp(m - m_new)
    p = jnp.exp(lg - m_new)
    return m_new, l * a + jnp.sum(p, axis=1, keepdims=True), acc * a + _dot(p.astype(BF16), values)


def _softmax_init():
    return jnp.full((8, 1), NEG, F32), jnp.zeros((8, 1), F32), jnp.zeros((8, LANE), F32)


def _softmax_finish(lg_new, row_new, state):
    m, l, acc = state
    m_new = jnp.maximum(m, lg_new)
    a = jnp.exp(m - m_new)
    p = jnp.exp(lg_new - m_new)
    return (acc * a + p * row_new) / (l * a + p)


def _sample_a_kernel(pt_ref, rb_ref, q_ref, kvn_ref, msk_ref, cache_ref, o_ref, buf, sem, *, layer, n_pages, ch):
    b = pl.program_id(0)
    q = q_ref[0]
    lane = lax.broadcasted_iota(I32, (1, LANE), 1)
    far, last, new = [], [], []
    for h in range(8):
        hh = min(h, A_HEADS - 1)
        far.append(jnp.full((1, LANE), rb_ref[N_BUCKETS - 1, hh], F32))
        last.append(_t5_bias_tile(rb_ref, hh, PAGE_SIZE - lane))
        new.append(jnp.full((1, 1), rb_ref[0, hh], F32))
    far, last, new = (jnp.concatenate(v, axis=0) for v in (far, last, new))

    def compute(c, view, state):
        kv = view[...].reshape(ch * PAGE_SIZE, 2 * HEAD_DIM).astype(BF16)
        lg = _lane_tiles(_nt_dot(q, kv), ch)
        add = [msk_ref[c * ch + k, 0] + jnp.where(c * ch + k == n_pages - 1, last, far) for k in range(ch)]
        lg = jnp.concatenate([t + a for t, a in zip(lg, add)], axis=1)
        return _softmax_step(lg, kv, state)

    state = _paged_loop(pt_ref, b, cache_ref, layer, buf, sem, n_pages // ch, ch, False, compute, _softmax_init())
    row_new = kvn_ref[0].astype(F32)
    lg_new = jnp.sum(q.astype(F32) * row_new, axis=1, keepdims=True) + new + msk_ref[n_pages, 0][:, 0:1]
    o_ref[0] = _softmax_finish(lg_new, row_new, state)


def _sample_a(page_table, cache, layer, rel_bias, q, kv_new, mask, ch):
    db, n_pages = page_table.shape
    per_b = lambda shape: pl.BlockSpec((1,) + shape, lambda b, pt: (b, 0, 0))
    mask4 = mask.reshape(n_pages + 1, db, 1, LANE)
    return _paged_call(
        functools.partial(_sample_a_kernel, layer=layer, n_pages=n_pages, ch=ch), "sample_a",
        page_table, cache, (rel_bias, q, kv_new, mask4),
        [pl.BlockSpec(memory_space=pltpu.SMEM), per_b((8, LANE)), per_b((1, LANE)),
         pl.BlockSpec((n_pages + 1, 1, 1, LANE), lambda b, pt: (0, b, 0, 0))],
        jax.ShapeDtypeStruct((db, 8, LANE), F32), per_b((8, LANE)), ch)


def _sample_b_kernel(pt_ref, q_ref, sfx_ref, cross_ref, cache_ref, o_ref, buf, sem, *, layer, n_pages, ch):
    b = pl.program_id(0)
    q = q_ref[0]

    def split(x):
        hi = x.astype(BF16)
        return hi, (x - hi.astype(F32)).astype(BF16)

    def compute(c, view, carry):
        right, acc = carry
        kv = view[...].reshape(ch * PAGE_SIZE, 4 * HEAD_DIM).astype(BF16)
        z = jnp.concatenate(_lane_tiles(_nt_dot(q, kv[:, :2 * HEAD_DIM]), ch), axis=0)
        ls = _log_sigmoid(z)
        hi, lo = split(ls - z)
        cum = _dot(hi, sfx_ref[...]) + _dot(lo, sfx_ref[...])
        tot = cum[:, LANE:]
        thi, tlo = split(tot)
        off = _dot(cross_ref[...], thi) + _dot(cross_ref[...], tlo)
        w = jnp.exp(ls + cum[:, :LANE] + off + jnp.concatenate([right] * ch, axis=0))
        wl = jnp.concatenate([w[k * 8:(k + 1) * 8] for k in range(ch)], axis=1).astype(BF16)
        return right + off[:8] + tot[:8], acc + _dot(wl, kv[:, 2 * HEAD_DIM:])

    init = (jnp.zeros((8, LANE), F32), jnp.zeros((8, LANE), F32))
    _, acc = _paged_loop(pt_ref, b, cache_ref, layer, buf, sem, n_pages // ch, ch, True, compute, init)
    o_ref[0] = acc


def _sample_b(page_table, cache, layer, q, sfx, cross, ch):
    db, _ = page_table.shape
    per_b = lambda shape: pl.BlockSpec((1,) + shape, lambda b, pt: (b, 0, 0))
    full2 = lambda a: pl.BlockSpec(a.shape, lambda b, pt: (0, 0))
    return _paged_call(
        functools.partial(_sample_b_kernel, layer=layer, n_pages=page_table.shape[1], ch=ch), "sample_b",
        page_table, cache, (q, sfx, cross), [per_b((8, LANE)), full2(sfx), full2(cross)],
        jax.ShapeDtypeStruct((db, 8, LANE), F32), per_b((8, LANE)), ch)


def _sample_c_kernel(pt_ref, q_ref, rown_ref, cache_ref, o_ref, buf, sem, *, layer, n_pages, ch):
    b = pl.program_id(0)
    q = q_ref[0]
    width = C_KV_RANK + C_ROPE_DIM

    def compute(c, view, state):
        rows = view[...].reshape(ch * PAGE_SIZE, width).astype(BF16)
        lg = _nt_dot(q[:, :C_KV_RANK], rows[:, :C_KV_RANK]) + _nt_dot(q[:, C_KV_RANK:width], rows[:, C_KV_RANK:])
        return _softmax_step(lg, rows[:, :C_KV_RANK], state)

    state = _paged_loop(pt_ref, b, cache_ref, layer, buf, sem, n_pages // ch, ch, False, compute, _softmax_init())
    row_new = rown_ref[0].astype(F32)
    lg_new = jnp.sum(q.astype(F32) * row_new, axis=1, keepdims=True)
    o_ref[0] = _softmax_finish(lg_new, row_new[:, :C_KV_RANK], state)


def _sample_c(page_table, cache, layer, q, row_new, ch):
    db, _ = page_table.shape
    per_b = lambda shape: pl.BlockSpec((1,) + shape, lambda b, pt: (b, 0, 0))
    return _paged_call(
        functools.partial(_sample_c_kernel, layer=layer, n_pages=page_table.shape[1], ch=ch), "sample_c",
        page_table, cache, (q, row_new), [per_b((8, 2 * LANE)), per_b((1, 2 * LANE))],
        jax.ShapeDtypeStruct((db, 8, LANE), F32), per_b((8, LANE)), ch)


def _c_up_kernel(ol_ref, wuv_ref, o_ref):
    o_ref[...] = jnp.concatenate([_dot(ol_ref[:, h, :].astype(BF16), wuv_ref[h]) for h in range(C_HEADS)], axis=1)


def _c_up(o_lat, wuv):
    db = o_lat.shape[0]
    return pl.pallas_call(
        _c_up_kernel,
        out_shape=jax.ShapeDtypeStruct((db, C_HEADS * HEAD_DIM), F32),
        name="sample_c_up",
    )(o_lat, wuv)


def _cross_page_matrix(ch):
    idx = jnp.arange(ch * 8)
    return ((idx[:, None] % 8 == idx[None, :] % 8) & (idx[None, :] // 8 > idx[:, None] // 8)).astype(BF16)


def _layer_weights(l, w_in, c_q_norm, c_kv_norm, w_uq, w_uk, w_uv, w_out):
    wukt = jnp.transpose(w_uk[l], (1, 2, 0)).astype(BF16)
    wuv = jnp.transpose(w_uv[l], (1, 0, 2)).astype(BF16)
    proj = (_permute_w_in(w_in[l]), c_q_norm[l].reshape(1, -1), c_kv_norm[l].reshape(1, -1),
            _permute_w_uq(w_uq[l]), wukt, _rope_select())
    return proj, wuv, w_out[l].astype(BF16)


def _prompt_layer(hp, t_out, proj_w, wuv, w_out, lg, lb, tabs, rel_bias, pfx, sfx, n_sel, alpha):
    rows = hp.shape[1] // 4
    (aq, akv32, akv16, aiq, aik32, aik16, aiw, gates, bq, bkv32, bkvh, cqf, crow32, crow16) = _project(
        hp, t_out, rows, proj_w, tabs)
    oa = _attn_a(rel_bias, aiq, aiw, aq, aik16, akv16, pfx, n_sel)
    ob = _attn_b(bq, bkvh, sfx)
    oc = _attn_c(cqf, crow16, wuv)
    hp = _merge(hp, gates, oa, ob, oc, w_out, lg, lb, rows, alpha)
    return hp, (akv32, aik32, bkv32, crow32)


def _sample_layer(hs, l, proj_w, wuv, w_out, lg, lb, tabs, rel_bias, pfx, sfx, cross, n_sel, alpha,
                  page_table, cache_a_kv, cache_a_kidx, cache_b_kv, cache_c_kv, ch):
    db = hs.shape[0]
    (aq, akv32, akv16, aiq, aik32, aik16, aiw, gates, bq, bkv32, bkvh, cqf, crow32, crow16) = _project(
        hs[None], db, db, proj_w, tabs)
    by_seq = lambda a: jnp.transpose(a[0], (1, 0, 2))

    scores = _sample_idx(page_table, cache_a_kidx, l, by_seq(aiq), aiw[0, :, IDX_DIM:IDX_DIM + IDX_HEADS, None],
                         aik16[0, :, None, :IDX_DIM], ch)
    mask = _sample_topk(scores, pfx, n_sel)
    qa = jnp.pad(by_seq(aq), ((0, 0), (0, 8 - A_HEADS), (0, LANE - HEAD_DIM)))
    oa = _sample_a(page_table, cache_a_kv, l, rel_bias, qa, akv16[0, :, None, :], mask, ch)
    oa = oa[:, :A_HEADS, HEAD_DIM:].reshape(db, A_HEADS * HEAD_DIM)

    qb = by_seq(bq)
    zb = jnp.zeros_like(qb[:, :2])
    qb = jnp.concatenate([jnp.concatenate([qb[:, :2], zb], axis=2), jnp.concatenate([zb, qb[:, 2:]], axis=2),
                          jnp.zeros((db, 8 - B_HEADS, LANE), BF16)], axis=1)
    ob = _sample_b(page_table, cache_b_kv, l, qb, sfx, cross, ch)
    ob = jnp.concatenate([ob[:, 0, :HEAD_DIM], ob[:, 1, :HEAD_DIM], ob[:, 2, HEAD_DIM:], ob[:, 3, HEAD_DIM:]], axis=1)

    o_lat = _sample_c(page_table, cache_c_kv, l, by_seq(cqf), crow16[0, :, None, :], ch)
    oc = _c_up(o_lat, wuv)
    hs = _merge(hs[None], gates, oa[None], ob[None], oc[None], w_out, lg, lb, db, alpha)[0]
    return hs, (akv32[0], aik32[0], bkv32[0], crow32[0])


def kernel(x_prompt, x_sample, cache_a_kv, cache_a_kidx, cache_b_kv, cache_c_kv, page_table, meta_tokens, rel_bias,
           emb_ln_g, emb_ln_b, w_in, c_q_norm, c_kv_norm, w_uq, w_uk, w_uv, w_out, ln_g, ln_b):
    bp, seq, d = x_prompt.shape
    db, dec_seq, _ = x_sample.shape
    assert dec_seq == 1
    depth = w_in.shape[0]
    n_phys = cache_a_kv.shape[0]
    n_pages = page_table.shape[1]
    past = n_pages * PAGE_SIZE
    t_p = N_META + seq
    t_pad = -(-t_p // TILE) * TILE
    n_sel_p = min(TOPK_MAX, seq // 4)
    n_sel_s = min(TOPK_MAX, (past + dec_seq) // 4)
    alpha = (2.0 * depth) ** 0.25
    ch = min(16, n_pages // 2)

    meta = jnp.broadcast_to(meta_tokens[None], (bp, N_META, d))
    xp = jnp.concatenate([meta, x_prompt, jnp.zeros((bp, t_pad - t_p, d), F32)], axis=1).reshape(bp * t_pad, d)
    ln_rows = 4 * TILE if (bp * t_pad) % (4 * TILE) == 0 else TILE
    hp = _layernorm(xp, emb_ln_g, emb_ln_b, ln_rows).reshape(bp, t_pad, d)
    hs = _layernorm(x_sample.reshape(db, d), emb_ln_g, emb_ln_b, db)

    tabs_p = _rope_tables(jnp.arange(t_pad, dtype=I32))
    tabs_s = _rope_tables(jnp.full((db,), past, I32))
    pfx, sfx = _prefix_matrices()
    cross = _cross_page_matrix(ch)
    ca_kv = cache_a_kv.reshape(n_phys, depth, PAGE_SIZE, 2 * HEAD_DIM)
    cb_kv = cache_b_kv.reshape(n_phys, depth, PAGE_SIZE, 2 * B_KV_HEADS * HEAD_DIM)

    rows_p, rows_s = [], []
    for l in range(depth):
        proj_w, wuv, wo = _layer_weights(l, w_in, c_q_norm, c_kv_norm, w_uq, w_uk, w_uv, w_out)
        hp, rp = _prompt_layer(hp, t_p, proj_w, wuv, wo, ln_g[l], ln_b[l], tabs_p, rel_bias, pfx, sfx, n_sel_p, alpha)
        hs, rs = _sample_layer(hs, l, proj_w, wuv, wo, ln_g[l], ln_b[l], tabs_s, rel_bias, pfx, sfx, cross, n_sel_s,
                               alpha, page_table, ca_kv, cache_a_kidx, cb_kv, cache_c_kv, ch)
        rows_p.append(rp)
        rows_s.append(rs)

    def stack(rows, k, lead, tail):
        return jnp.stack([r[k].reshape(lead + tail) for r in rows], axis=1)

    lp, ls = (bp, t_p), (db, dec_seq)
    return (hp[:, N_META:t_p], hs[:, None, :],
            stack(rows_p, 0, lp, (2, HEAD_DIM)), stack(rows_p, 1, lp, (IDX_DIM,)),
            stack(rows_p, 2, lp, (2, B_KV_HEADS, HEAD_DIM)), stack(rows_p, 3, lp, (C_KV_RANK + C_ROPE_DIM,)),
            stack(rows_s, 0, ls, (2, HEAD_DIM)), stack(rows_s, 1, ls, (IDX_DIM,)),
            stack(rows_s, 2, ls, (2, B_KV_HEADS, HEAD_DIM)), stack(rows_s, 3, ls, (C_KV_RANK + C_ROPE_DIM,)))
```

```python
import functools
import math

import jax
import jax.numpy as jnp
from jax import lax
from jax.experimental import pallas as pl
from jax.experimental.pallas import tpu as pltpu

F32, BF16, I32 = jnp.float32, jnp.bfloat16, jnp.int32

N_META = 16
HEAD_DIM = 64
A_HEADS = 4
IDX_HEADS = 8
IDX_DIM = 64
TOPK_MAX = 256
B_HEADS = 4
B_KV_HEADS = 2
C_HEADS = 8
C_NOPE_DIM = 64
C_ROPE_DIM = 32
C_KV_RANK = 128
N_BUCKETS = 32
MAX_DISTANCE = 128
ROPE_THETA = 10000.0
PAGE_SIZE = 128
LN_EPS = 1e-5
RMS_EPS = 1e-6

LANE = 128
TILE = 128
VMEM_LIMIT = 56 * 1024 * 1024
NEG = -1e30
SURVIVAL_CUTOFF = -100.0
INT_MIN = -2 ** 31
C_SCALE = (C_NOPE_DIM + C_ROPE_DIM) ** -0.5
IW_SCALE = IDX_HEADS ** -0.5 * IDX_DIM ** -0.5

SEG_AQ = (0, 256)
SEG_AKV = (256, 384)
SEG_AIQ = (384, 896)
SEG_AIK = (896, 1024)
SEG_G = (1024, 2048)
SEG_BQ = (2048, 2304)
SEG_BKV = (2304, 2560)
SEG_CQ = (2560, 2816)
SEG_CKV = (2816, 2944)
SEG_CKR = (2944, 3072)
IN_PERM_WIDTH = 3072


def _params(*sem):
    return pltpu.CompilerParams(dimension_semantics=sem, vmem_limit_bytes=VMEM_LIMIT)


def _nt_dot(a, b):
    return lax.dot_general(a, b, (((1,), (1,)), ((), ())), preferred_element_type=F32)


def _dot(a, b):
    return jnp.dot(a, b, preferred_element_type=F32)


def _ln_kernel(x_ref, g_ref, b_ref, o_ref):
    x = x_ref[...]
    mu = jnp.mean(x, -1, keepdims=True)
    xc = x - mu
    var = jnp.mean(xc * xc, -1, keepdims=True)
    o_ref[...] = xc * lax.rsqrt(var + LN_EPS) * g_ref[...] + b_ref[...]


def _layernorm(x2d, g, b, rows):
    n, d = x2d.shape
    return pl.pallas_call(
        _ln_kernel,
        out_shape=jax.ShapeDtypeStruct((n, d), F32),
        grid=(n // rows,),
        in_specs=[pl.BlockSpec((rows, d), lambda i: (i, 0)),
                  pl.BlockSpec((1, d), lambda i: (0, 0)),
                  pl.BlockSpec((1, d), lambda i: (0, 0))],
        out_specs=pl.BlockSpec((rows, d), lambda i: (i, 0)),
        compiler_params=_params("arbitrary"),
        name="embed_ln",
    )(x2d, g.reshape(1, d), b.reshape(1, d))


def _project_kernel(x_ref, w_ref, qn_ref, kvn_ref, wuq_ref, wukt_ref, sel_ref,
                    cq_ref, sq_ref, ck_ref, sa_ref, sb_ref,
                    aq_ref, akv32_ref, akv16_ref, aiq_ref, aik32_ref, aik16_ref, aiw_ref, g_ref,
                    bq_ref, bkv32_ref, bkvh_ref, cqf_ref, crow32_ref, crow16_ref):
    xb = x_ref[0].astype(BF16)

    def seg(ab):
        return _dot(xb, w_ref[:, ab[0]:ab[1]])

    s = seg(SEG_AQ)
    for h in range(A_HEADS):
        aq_ref[0, h] = (s[:, h * 64:(h + 1) * 64] * HEAD_DIM ** -0.5).astype(BF16)
    s = seg(SEG_AKV)
    akv32_ref[0] = s
    akv16_ref[0] = s.astype(BF16)
    s = seg(SEG_AIQ)
    for h in range(IDX_HEADS):
        aiq_ref[0, h] = s[:, h * 64:(h + 1) * 64].astype(BF16)
    s = seg(SEG_AIK)
    aik32_ref[0] = s[:, :IDX_DIM]
    aik16_ref[0] = s.astype(BF16)
    aiw_ref[0] = s * IW_SCALE
    g_ref[0] = seg(SEG_G)
    s = seg(SEG_BQ)
    for h in range(B_HEADS):
        bq_ref[0, h] = (s[:, h * 64:(h + 1) * 64] * HEAD_DIM ** -0.5).astype(BF16)
    s = seg(SEG_BKV)
    bkv32_ref[0] = s
    for h in range(2 * B_KV_HEADS):
        bkvh_ref[0, h] = s[:, h * 64:(h + 1) * 64].astype(BF16)

    cq = seg(SEG_CQ)
    cqn = cq * lax.rsqrt(jnp.mean(cq * cq, -1, keepdims=True) + RMS_EPS) * qn_ref[...]
    cu = _dot(cqn.astype(BF16), wuq_ref[...])
    r1, r2 = cu[:, 512:640], cu[:, 640:768]
    c, sn = cq_ref[...], sq_ref[...]
    rot = (jnp.concatenate([r1 * c - r2 * sn, r1 * sn + r2 * c], axis=1) * C_SCALE).astype(BF16)
    for h in range(C_HEADS):
        qlat = _dot(cu[:, h * 64:(h + 1) * 64].astype(BF16), wukt_ref[h]) * C_SCALE
        qrope = _dot(rot, sel_ref[h])
        cqf_ref[0, h] = jnp.concatenate([qlat, qrope], axis=1).astype(BF16)

    ckv = seg(SEG_CKV)
    ckvn = ckv * lax.rsqrt(jnp.mean(ckv * ckv, -1, keepdims=True) + RMS_EPS) * kvn_ref[...]
    kr = seg(SEG_CKR)
    krot = kr * ck_ref[...] + pltpu.roll(kr, LANE - 16, 1) * sa_ref[...] + pltpu.roll(kr, 16, 1) * sb_ref[...]
    crow32_ref[0, :, 0:C_KV_RANK] = ckvn
    crow32_ref[0, :, C_KV_RANK:C_KV_RANK + C_ROPE_DIM] = krot[:, :C_ROPE_DIM]
    crow16_ref[0] = jnp.concatenate([ckvn, krot], axis=1).astype(BF16)


def _project(x, t_out, rows, wts, tabs):
    bn, t, d = x.shape
    nt = t // rows
    w_in, qn, kvn, wuq, wukt, sel = wts

    def full(a):
        nd = a.ndim
        return pl.BlockSpec(a.shape, lambda b, i, _n=nd: (0,) * _n)

    def rows3(w):
        return pl.BlockSpec((1, rows, w), lambda b, i: (b, i, 0))

    def heads4(nh, w):
        return pl.BlockSpec((1, nh, rows, w), lambda b, i: (b, 0, i, 0))

    tab_spec = pl.BlockSpec((rows, LANE), lambda b, i: (i, 0))
    out_shape = [
        jax.ShapeDtypeStruct((bn, A_HEADS, t, 64), BF16),
        jax.ShapeDtypeStruct((bn, t_out, 128), F32),
        jax.ShapeDtypeStruct((bn, t, 128), BF16),
        jax.ShapeDtypeStruct((bn, IDX_HEADS, t, 64), BF16),
        jax.ShapeDtypeStruct((bn, t_out, 64), F32),
        jax.ShapeDtypeStruct((bn, t, 128), BF16),
        jax.ShapeDtypeStruct((bn, t, 128), F32),
        jax.ShapeDtypeStruct((bn, t, 1024), F32),
        jax.ShapeDtypeStruct((bn, B_HEADS, t, 64), BF16),
        jax.ShapeDtypeStruct((bn, t_out, 256), F32),
        jax.ShapeDtypeStruct((bn, 4, t, 64), BF16),
        jax.ShapeDtypeStruct((bn, C_HEADS, t, 256), BF16),
        jax.ShapeDtypeStruct((bn, t_out, 160), F32),
        jax.ShapeDtypeStruct((bn, t, 256), BF16),
    ]
    out_specs = [heads4(A_HEADS, 64), rows3(128), rows3(128), heads4(IDX_HEADS, 64), rows3(64), rows3(128),
                 rows3(128), rows3(1024), heads4(B_HEADS, 64), rows3(256), heads4(4, 64),
                 heads4(C_HEADS, 256), rows3(160), rows3(256)]
    return pl.pallas_call(
        _project_kernel,
        out_shape=out_shape,
        grid=(bn, nt),
        in_specs=[rows3(d), full(w_in), full(qn), full(kvn), full(wuq), full(wukt), full(sel)] + [tab_spec] * 5,
        out_specs=out_specs,
        compiler_params=_params("arbitrary", "arbitrary"),
        name="project",
    )(x, w_in, qn, kvn, wuq, wukt, sel, *tabs)


def _merge_kernel(x_ref, g_ref, oa_ref, ob_ref, oc_ref, w_ref, lg_ref, lb_ref, y_ref, *, alpha):
    g = g_ref[0]
    o = jnp.concatenate([oa_ref[0], ob_ref[0], oc_ref[0]], axis=1)
    mixed = (o * (g * jax.nn.sigmoid(g))).astype(BF16)
    z = alpha * x_ref[0] + _dot(mixed, w_ref[...])
    mu = jnp.mean(z, -1, keepdims=True)
    zc = z - mu
    var = jnp.mean(zc * zc, -1, keepdims=True)
    y_ref[0] = zc * lax.rsqrt(var + LN_EPS) * lg_ref[...] + lb_ref[...]


def _merge(x, gates, oa, ob, oc, w_out, lg, lb, rows, alpha):
    bn, t, d = x.shape

    def rows3(w):
        return pl.BlockSpec((1, rows, w), lambda b, i: (b, i, 0))

    def full(a):
        nd = a.ndim
        return pl.BlockSpec(a.shape, lambda b, i, _n=nd: (0,) * _n)

    lg2, lb2 = lg.reshape(1, d), lb.reshape(1, d)
    return pl.pallas_call(
        functools.partial(_merge_kernel, alpha=alpha),
        out_shape=jax.ShapeDtypeStruct((bn, t, d), F32),
        grid=(bn, t // rows),
        in_specs=[rows3(d), rows3(1024), rows3(256), rows3(256), rows3(512), full(w_out), full(lg2), full(lb2)],
        out_specs=rows3(d),
        compiler_params=_params("arbitrary", "arbitrary"),
        name="merge",
    )(x, gates, oa, ob, oc, w_out, lg2, lb2)


def _t5_bias_tile(rb_ref, h, dist):
    max_exact = N_BUCKETS // 2
    lg = jnp.log(jnp.maximum(dist, 1).astype(F32) / max_exact) / math.log(MAX_DISTANCE / max_exact)
    large = jnp.minimum(max_exact + (lg * (N_BUCKETS - max_exact)).astype(I32), N_BUCKETS - 1)
    bucket = jnp.where(dist < max_exact, dist, large)
    out = jnp.zeros(dist.shape, F32)
    for k in range(N_BUCKETS):
        out = jnp.where(bucket == k, rb_ref[k, h], out)
    return out


def _sortable(score):
    bits = pltpu.bitcast(score + 0.0, I32)
    return bits ^ ((bits >> 31) & 0x7FFFFFFF)


def _kth_largest(count_ge, shape, k):
    def body(t, tau):
        bit = lax.shift_left(jnp.int32(1), 31 - t)
        cand = tau + bit
        return jnp.where(count_ge(cand) >= k, cand, tau)

    return lax.fori_loop(0, 32, body, jnp.full(shape, INT_MIN, I32))


def _attn_a_kernel(rb_ref, iq_ref, iw_ref, q_ref, ik_ref, kv_ref, pfx_ref, o_ref,
                   bias_scr, wb_scr, key_scr, msk_scr, lg_scr, mx_scr, acc_scr, ls_scr, *, n_sel):
    b, i = pl.program_id(0), pl.program_id(1)
    nkt = i + 1
    row = lax.broadcasted_iota(I32, (TILE, TILE), 0)
    col = lax.broadcasted_iota(I32, (TILE, TILE), 1)

    @pl.when((b == 0) & (i == 0))
    def _():
        for h in range(A_HEADS):
            bias_scr[h, 0] = _t5_bias_tile(rb_ref, h, jnp.maximum(row - col, 0))
            bias_scr[h, 1] = _t5_bias_tile(rb_ref, h, TILE + row - col)
            bias_scr[h, 2] = _t5_bias_tile(rb_ref, h, jnp.full((TILE, TILE), 2 * TILE, I32))

    iw = iw_ref[0]
    for h in range(IDX_HEADS):
        wb_scr[h] = jnp.broadcast_to(iw[:, IDX_DIM + h:IDX_DIM + h + 1], (TILE, TILE))
    iqs = iq_ref[0].reshape(IDX_HEADS * TILE, IDX_DIM)

    def score_tile(j, carry):
        kt = ik_ref[0, pl.ds(pl.multiple_of(j * TILE, TILE), TILE), :][:, :IDX_DIM]
        s = _nt_dot(iqs, kt)
        acc = jnp.zeros((TILE, TILE), F32)
        for h in range(IDX_HEADS):
            acc = acc + wb_scr[h] * jnp.maximum(s[h * TILE:(h + 1) * TILE], 0.0)
        key_scr[j] = _sortable(acc)
        return carry

    lax.fori_loop(0, nkt, score_tile, 0)
    key_scr[i] = jnp.where(col <= row, key_scr[i], INT_MIN)

    def count_ge(cand):
        cb = jnp.broadcast_to(cand, (TILE, TILE))
        c = lax.fori_loop(0, nkt, lambda j, c: c + jnp.where(key_scr[j] >= cb, 1, 0),
                          jnp.zeros((TILE, TILE), I32))
        return jnp.sum(c, axis=1, keepdims=True)

    tau = _kth_largest(count_ge, (TILE, 1), n_sel)
    tau_b = jnp.broadcast_to(jnp.maximum(tau, INT_MIN + 1), (TILE, TILE))
    n_gt = jnp.sum(lax.fori_loop(0, nkt, lambda j, c: c + jnp.where(key_scr[j] > tau_b, 1, 0),
                                 jnp.zeros((TILE, TILE), I32)), axis=1, keepdims=True)
    take_b = jnp.broadcast_to((n_sel - n_gt).astype(F32), (TILE, TILE))

    def mask_tile(j, seen):
        kt = key_scr[j]
        eq = kt == tau_b
        both = _dot(jnp.where(eq, 1.0, 0.0).astype(BF16), pfx_ref[...])
        sel = (kt > tau_b) | (eq & (both[:, :TILE] + seen < take_b))
        msk_scr[j] = jnp.where(sel, 0.0, NEG)
        return seen + both[:, TILE:]

    lax.fori_loop(0, nkt, mask_tile, jnp.zeros((TILE, TILE), F32))

    qs = q_ref[0].reshape(A_HEADS * TILE, HEAD_DIM)
    mx_scr[...] = jnp.full(mx_scr.shape, NEG, F32)

    def logits_tile(j, carry):
        kvt = kv_ref[0, pl.ds(pl.multiple_of(j * TILE, TILE), TILE), :]
        lg = _nt_dot(qs, kvt[:, :HEAD_DIM])
        bidx = jnp.minimum(i - j, 2)
        madd = msk_scr[j]
        for h in range(A_HEADS):
            sl = slice(h * TILE, (h + 1) * TILE)
            lgh = lg[sl] + bias_scr[h, bidx] + madd
            lg_scr[j, sl] = lgh
            mx_scr[sl] = jnp.maximum(mx_scr[sl], lgh)
        return carry

    lax.fori_loop(0, nkt, logits_tile, 0)
    m_b = jnp.broadcast_to(jnp.max(mx_scr[...], axis=1, keepdims=True), mx_scr.shape)
    acc_scr[...] = jnp.zeros(acc_scr.shape, F32)
    ls_scr[...] = jnp.zeros(ls_scr.shape, F32)

    def pv_tile(j, carry):
        kvt = kv_ref[0, pl.ds(pl.multiple_of(j * TILE, TILE), TILE), :]
        p = jnp.exp(lg_scr[j] - m_b)
        ls_scr[...] += p
        acc_scr[...] += _dot(p.astype(BF16), kvt)
        return carry

    lax.fori_loop(0, nkt, pv_tile, 0)
    o = acc_scr[...] / jnp.sum(ls_scr[...], axis=1, keepdims=True)
    o_ref[0] = jnp.concatenate([o[h * TILE:(h + 1) * TILE, HEAD_DIM:] for h in range(A_HEADS)], axis=1)


def _attn_a(rel_bias, aiq, aiw, aq, aik16, akv16, pfx, n_sel):
    bn, _, t, _ = aq.shape
    nt = t // TILE
    return pl.pallas_call(
        functools.partial(_attn_a_kernel, n_sel=n_sel),
        out_shape=jax.ShapeDtypeStruct((bn, t, A_HEADS * HEAD_DIM), F32),
        grid=(bn, nt),
        in_specs=[pl.BlockSpec(memory_space=pltpu.SMEM),
                  pl.BlockSpec((1, IDX_HEADS, TILE, 64), lambda b, i: (b, 0, i, 0)),
                  pl.BlockSpec((1, TILE, 128), lambda b, i: (b, i, 0)),
                  pl.BlockSpec((1, A_HEADS, TILE, 64), lambda b, i: (b, 0, i, 0)),
                  pl.BlockSpec((1, t, 128), lambda b, i: (b, 0, 0)),
                  pl.BlockSpec((1, t, 128), lambda b, i: (b, 0, 0)),
                  pl.BlockSpec((TILE, 2 * TILE), lambda b, i: (0, 0))],
        out_specs=pl.BlockSpec((1, TILE, A_HEADS * HEAD_DIM), lambda b, i: (b, i, 0)),
        scratch_shapes=[pltpu.VMEM((A_HEADS, 3, TILE, TILE), F32),
                        pltpu.VMEM((IDX_HEADS, TILE, TILE), F32),
                        pltpu.VMEM((nt, TILE, TILE), I32),
                        pltpu.VMEM((nt, TILE, TILE), F32),
                        pltpu.VMEM((nt, A_HEADS * TILE, TILE), F32),
                        pltpu.VMEM((A_HEADS * TILE, TILE), F32),
                        pltpu.VMEM((A_HEADS * TILE, TILE), F32),
                        pltpu.VMEM((A_HEADS * TILE, TILE), F32)],
        compiler_params=_params("arbitrary", "arbitrary"),
        name="attn_a_prompt",
    )(rel_bias, aiq, aiw, aq, aik16, akv16, pfx)


def _log_sigmoid(z):
    return jnp.minimum(z, 0.0) - jnp.log1p(jnp.exp(-jnp.abs(z)))


def _attn_b_kernel(q_ref, kv_ref, sfx_ref, o_ref, right_scr, acc_scr):
    i = pl.program_id(1)
    rows = (B_HEADS // B_KV_HEADS) * TILE
    row = lax.broadcasted_iota(I32, (rows, TILE), 0) & (TILE - 1)
    col = lax.broadcasted_iota(I32, (rows, TILE), 1)
    qs = [q_ref[0, 2 * g:2 * g + 2].reshape(rows, HEAD_DIM) for g in range(B_KV_HEADS)]

    def tile(j, diagonal):
        ks = pl.ds(pl.multiple_of(j * TILE, TILE), TILE)
        for g in range(B_KV_HEADS):
            z = _nt_dot(qs[g], kv_ref[0, g, ks, :])
            ls = _log_sigmoid(z)
            lneg = ls - z
            if diagonal:
                lneg = jnp.where(col < row, lneg, 0.0)
            hi, lo = _split_bf16(lneg)
            cum = _dot(hi, sfx_ref[...]) + _dot(lo, sfx_ref[...])
            w = jnp.exp(ls + cum[:, :TILE] + right_scr[g])
            if diagonal:
                w = jnp.where(col < row, w, 0.0)
            acc_scr[g] += _dot(w.astype(BF16), kv_ref[0, B_KV_HEADS + g, ks, :])
            right_scr[g] += cum[:, TILE:]

    right_scr[...] = jnp.zeros(right_scr.shape, F32)
    acc_scr[...] = jnp.zeros(acc_scr.shape, F32)
    tile(i, True)

    def more(j):
        return (j >= 0) & (jnp.max(right_scr[...]) > SURVIVAL_CUTOFF)

    def older(j):
        tile(j, False)
        return j - 1

    lax.while_loop(more, older, i - 1)
    o_ref[0] = jnp.concatenate([acc_scr[g, r * TILE:(r + 1) * TILE] for g in range(B_KV_HEADS) for r in range(2)],
                               axis=1)


def _attn_b(bq, bkvh, sfx):
    bn, _, t, _ = bq.shape
    return pl.pallas_call(
        _attn_b_kernel,
        out_shape=jax.ShapeDtypeStruct((bn, t, B_HEADS * HEAD_DIM), F32),
        grid=(bn, t // TILE),
        in_specs=[pl.BlockSpec((1, B_HEADS, TILE, 64), lambda b, i: (b, 0, i, 0)),
                  pl.BlockSpec((1, 4, t, 64), lambda b, i: (b, 0, 0, 0)),
                  pl.BlockSpec((TILE, 2 * TILE), lambda b, i: (0, 0))],
        out_specs=pl.BlockSpec((1, TILE, B_HEADS * HEAD_DIM), lambda b, i: (b, i, 0)),
        scratch_shapes=[pltpu.VMEM((B_KV_HEADS, 2 * TILE, TILE), F32),
                        pltpu.VMEM((B_KV_HEADS, 2 * TILE, HEAD_DIM), F32)],
        compiler_params=_params("arbitrary", "arbitrary"),
        name="attn_b_prompt",
    )(bq, bkvh, sfx)


def _attn_c_kernel(q_ref, c_ref, wuv_ref, o_ref, lg_scr, mx_scr, acc_scr, ls_scr):
    i = pl.program_id(1)
    rows = C_HEADS * TILE
    row = lax.broadcasted_iota(I32, (rows, TILE), 0) & (TILE - 1)
    col = lax.broadcasted_iota(I32, (rows, TILE), 1)
    qs = q_ref[0].reshape(rows, 2 * LANE)
    mx_scr[...] = jnp.full(mx_scr.shape, NEG, F32)

    def logits_tile(j, carry):
        ct = c_ref[0, pl.ds(pl.multiple_of(j * TILE, TILE), TILE), :]
        lg = jnp.where((col <= row) | (j < i), _nt_dot(qs, ct), NEG)
        lg_scr[j] = lg
        mx_scr[...] = jnp.maximum(mx_scr[...], lg)
        return carry

    lax.fori_loop(0, i + 1, logits_tile, 0)
    m_b = jnp.broadcast_to(jnp.max(mx_scr[...], axis=1, keepdims=True), (rows, TILE))
    acc_scr[...] = jnp.zeros(acc_scr.shape, F32)
    ls_scr[...] = jnp.zeros(ls_scr.shape, F32)

    def pv_tile(j, carry):
        ct = c_ref[0, pl.ds(pl.multiple_of(j * TILE, TILE), TILE), :]
        p = jnp.exp(lg_scr[j] - m_b)
        ls_scr[...] += p
        acc_scr[...] += _dot(p.astype(BF16), ct[:, :C_KV_RANK])
        return carry

    lax.fori_loop(0, i + 1, pv_tile, 0)
    ol = (acc_scr[...] / jnp.sum(ls_scr[...], axis=1, keepdims=True)).astype(BF16)
    o_ref[0] = jnp.concatenate([_dot(ol[h * TILE:(h + 1) * TILE], wuv_ref[h]) for h in range(C_HEADS)], axis=1)


def _attn_c(cqf, crow16, wuv):
    bn, _, t, _ = cqf.shape
    nt = t // TILE
    rows = C_HEADS * TILE
    return pl.pallas_call(
        _attn_c_kernel,
        out_shape=jax.ShapeDtypeStruct((bn, t, C_HEADS * HEAD_DIM), F32),
        grid=(bn, nt),
        in_specs=[pl.BlockSpec((1, C_HEADS, TILE, 256), lambda b, i: (b, 0, i, 0)),
                  pl.BlockSpec((1, t, 256), lambda b, i: (b, 0, 0)),
                  pl.BlockSpec(wuv.shape, lambda b, i: (0, 0, 0))],
        out_specs=pl.BlockSpec((1, TILE, C_HEADS * HEAD_DIM), lambda b, i: (b, i, 0)),
        scratch_shapes=[pltpu.VMEM((nt, rows, TILE), F32),
                        pltpu.VMEM((rows, TILE), F32),
                        pltpu.VMEM((rows, TILE), F32),
                        pltpu.VMEM((rows, TILE), F32)],
        compiler_params=_params("arbitrary", "arbitrary"),
        name="attn_c_prompt",
    )(cqf, crow16, wuv)


def _permute_w_in(w):
    d = w.shape[0]
    names = dict(a_q=(0, 256), a_k=(256, 320), a_v=(320, 384), a_iq=(384, 896), a_ik=(896, 960), a_iw=(960, 968),
                 a_g=(968, 1224), b_q=(1224, 1480), b_k=(1480, 1608), b_v=(1608, 1736), b_g=(1736, 1992),
                 c_q=(1992, 2248), c_kv=(2248, 2376), c_kr=(2376, 2408), c_g=(2408, 2920))

    def c(n):
        return w[:, names[n][0]:names[n][1]]

    z = lambda n: jnp.zeros((d, n), w.dtype)
    out = jnp.concatenate([c("a_q"), c("a_k"), c("a_v"), c("a_iq"), c("a_ik"), c("a_iw"), z(56),
                           c("a_g"), c("b_g"), c("c_g"), c("b_q"), c("b_k"), c("b_v"),
                           c("c_q"), c("c_kv"), c("c_kr"), z(96)], axis=1)
    assert out.shape[1] == IN_PERM_WIDTH
    return out.astype(BF16)


def _permute_w_uq(w):
    r = w.reshape(w.shape[0], C_HEADS, C_NOPE_DIM + C_ROPE_DIM)
    half = C_ROPE_DIM // 2
    nope = r[:, :, :C_NOPE_DIM].reshape(w.shape[0], -1)
    lo = r[:, :, C_NOPE_DIM:C_NOPE_DIM + half].reshape(w.shape[0], -1)
    hi = r[:, :, C_NOPE_DIM + half:].reshape(w.shape[0], -1)
    return jnp.concatenate([nope, lo, hi], axis=1).astype(BF16)


def _rope_select():
    half = C_ROPE_DIM // 2
    src = jnp.arange(2 * LANE)[:, None]
    dst = jnp.arange(LANE)[None, :]
    mats = []
    for h in range(C_HEADS):
        lo = (src == h * half + dst) & (dst < half)
        hi = (src == LANE + h * half + dst - half) & (dst >= half) & (dst < 2 * half)
        mats.append(lo | hi)
    return jnp.stack(mats).astype(BF16)


def _rope_tables(pos):
    half = C_ROPE_DIM // 2
    freq = ROPE_THETA ** (-jnp.arange(half, dtype=F32) / half)
    ang = pos.astype(F32)[:, None] * freq
    c, s = jnp.cos(ang), jnp.sin(ang)
    cq, sq = jnp.tile(c, (1, LANE // half)), jnp.tile(s, (1, LANE // half))
    z = jnp.zeros((pos.shape[0], LANE - 2 * half), F32)
    zh = jnp.zeros_like(c)
    ck = jnp.concatenate([c, c, z], axis=1)
    sa = jnp.concatenate([-s, zh, z], axis=1)
    sb = jnp.concatenate([zh, s, z], axis=1)
    return cq, sq, ck, sa, sb


def _prefix_matrices():
    a = jnp.arange(TILE)
    ones = jnp.ones((TILE, TILE), F32)
    pfx = jnp.concatenate([(a[:, None] < a[None, :]).astype(F32), ones], axis=1).astype(BF16)
    sfx = jnp.concatenate([(a[:, None] > a[None, :]).astype(F32), ones], axis=1).astype(BF16)
    return pfx, sfx


def _page_copies(pt_ref, cache_ref, layer, buf, sem, slot, ch, page_of):
    return [pltpu.make_async_copy(cache_ref.at[page_of(k), layer], buf.at[slot, k], sem.at[slot]) for k in range(ch)]


def _paged_loop(pt_ref, cache_ref, layer, buf, sem, n_chunks, ch, compute, init):
    b, nb = pl.program_id(0), pl.num_programs(0)

    def start(seq, c, slot):
        for cp in _page_copies(pt_ref, cache_ref, layer, buf, sem, slot, ch, lambda k: pt_ref[seq, c * ch + k]):
            cp.start()

    def wait(slot):
        for cp in _page_copies(pt_ref, cache_ref, layer, buf, sem, slot, ch, lambda k: 0):
            cp.wait()

    @pl.when(b == 0)
    def _():
        start(0, 0, 0)

    def body(c, carry):
        slot = (b * n_chunks + c) & 1

        @pl.when(c + 1 < n_chunks)
        def _():
            start(b, c + 1, 1 - slot)

        @pl.when((c + 1 == n_chunks) & (b + 1 < nb))
        def _():
            start(b + 1, 0, 1 - slot)

        wait(slot)
        return compute(c, _chunk_slab(buf.at[slot], ch), carry)

    return lax.fori_loop(0, n_chunks, body, init)


def _chunk_slab(view, ch):
    return jnp.concatenate([view[k] for k in range(ch)], axis=1).astype(BF16)


def _lane_tiles(x, n):
    return [x[:, k * LANE:(k + 1) * LANE] for k in range(n)]


def _paged_call(kernel, name, page_table, layer_cache, operands, operand_specs, out_shape, out_spec, ch, slots=2):
    db = page_table.shape[0]
    feat = layer_cache.shape[2]
    grid_spec = pltpu.PrefetchScalarGridSpec(
        num_scalar_prefetch=1,
        grid=(db,),
        in_specs=list(operand_specs) + [pl.BlockSpec(memory_space=pl.ANY)],
        out_specs=out_spec,
        scratch_shapes=[pltpu.VMEM((slots, ch, feat, PAGE_SIZE), F32), pltpu.SemaphoreType.DMA((slots,))],
    )
    return pl.pallas_call(kernel, out_shape=out_shape, grid_spec=grid_spec,
                          compiler_params=_params("arbitrary"), name=name)(page_table, *operands, layer_cache)


def _sample_idx_kernel(pt_ref, iq_ref, w_ref, ikn_ref, cache_ref, out_ref, buf, sem, *, layer, n_pages, ch):
    iq = iq_ref[0]
    w = w_ref[0]

    def compute(c, keys_t, carry):
        sc = jnp.sum(jnp.maximum(_dot(iq, keys_t), 0.0) * w, axis=0, keepdims=True)
        for k, tile in enumerate(_lane_tiles(sc, ch)):
            out_ref[c * ch + k, 0] = tile
        return carry

    _paged_loop(pt_ref, cache_ref, layer, buf, sem, n_pages // ch, ch, compute, 0)
    s_new = jnp.sum(iq.astype(F32) * ikn_ref[0].astype(F32), axis=1, keepdims=True)
    s_new = jnp.sum(jnp.maximum(s_new, 0.0) * w, axis=0, keepdims=True)
    lane = lax.broadcasted_iota(I32, (1, LANE), 1)
    out_ref[n_pages, 0] = jnp.where(lane == 0, s_new, -jnp.inf)


def _sample_idx(page_table, cache, layer, iq, w, ik_new, ch):
    db, n_pages = page_table.shape
    per_b = lambda shape: pl.BlockSpec((1,) + shape, lambda b, pt: (b, 0, 0))
    out = _paged_call(
        functools.partial(_sample_idx_kernel, layer=layer, n_pages=n_pages, ch=ch), "sample_idx",
        page_table, cache, (iq, w, ik_new), [per_b((IDX_HEADS, IDX_DIM)), per_b((IDX_HEADS, 1)), per_b((1, IDX_DIM))],
        jax.ShapeDtypeStruct((n_pages + 1, db, 1, LANE), F32),
        pl.BlockSpec((n_pages + 1, 1, 1, LANE), lambda b, pt: (0, b, 0, 0)), ch)
    return out.reshape(n_pages + 1, db, LANE)


def _sample_topk_kernel(s_ref, pfx_ref, m_ref, key_scr, *, n_sel):
    nt, bb, _ = s_ref.shape

    def to_key(j, carry):
        key_scr[j] = _sortable(s_ref[j])
        return carry

    lax.fori_loop(0, nt, to_key, 0)

    def count(pred):
        c = lax.fori_loop(0, nt, lambda j, c: c + jnp.where(pred(key_scr[j]), 1, 0), jnp.zeros((bb, LANE), I32))
        return jnp.sum(c, axis=1, keepdims=True)

    def count_ge(cand):
        cb = jnp.broadcast_to(cand, (bb, LANE))
        return count(lambda kt: kt >= cb)

    tau = _kth_largest(count_ge, (bb, 1), n_sel)
    tau_b = jnp.broadcast_to(jnp.maximum(tau, INT_MIN + 1), (bb, LANE))
    take_b = jnp.broadcast_to((n_sel - count(lambda kt: kt > tau_b)).astype(F32), (bb, LANE))

    def mask_tile(j, seen):
        kt = key_scr[j]
        eq = kt == tau_b
        both = _dot(jnp.where(eq, 1.0, 0.0).astype(BF16), pfx_ref[...])
        sel = (kt > tau_b) | (eq & (both[:, :LANE] + seen < take_b))
        m_ref[j] = jnp.where(sel, 0.0, NEG)
        return seen + both[:, LANE:]

    lax.fori_loop(0, nt, mask_tile, jnp.zeros((bb, LANE), F32))


def _sample_topk(scores, pfx, n_sel):
    nt, db, _ = scores.shape
    bb = min(db, 32)
    spec = pl.BlockSpec((nt, bb, LANE), lambda i: (0, i, 0))
    return pl.pallas_call(
        functools.partial(_sample_topk_kernel, n_sel=n_sel),
        out_shape=jax.ShapeDtypeStruct((nt, db, LANE), F32),
        grid=(db // bb,),
        in_specs=[spec, pl.BlockSpec(pfx.shape, lambda i: (0, 0))],
        out_specs=spec,
        scratch_shapes=[pltpu.VMEM((nt, bb, LANE), I32)],
        compiler_params=_params("arbitrary"),
        name="sample_topk",
    )(scores, pfx)


def _softmax_step(lg, values_t, state):
    m, l, acc = state
    m_new = jnp.maximum(m, jnp.max(lg, axis=1, keepdims=True))
    a = jnp.exp(m - m_new)
    p = jnp.exp(lg - m_new)
    return m_new, l * a + jnp.sum(p, axis=1, keepdims=True), acc * a + _nt_dot(p.astype(BF16), values_t)


def _softmax_init():
    return jnp.full((8, 1), NEG, F32), jnp.zeros((8, 1), F32), jnp.zeros((8, LANE), F32)


def _softmax_finish(lg_new, row_new, state):
    m, l, acc = state
    m_new = jnp.maximum(m, lg_new)
    a = jnp.exp(m - m_new)
    p = jnp.exp(lg_new - m_new)
    return (acc * a + p * row_new) / (l * a + p)


def _sample_a_kernel(pt_ref, rb_ref, q_ref, kvn_ref, msk_ref, cache_ref, o_ref, buf, sem, *, layer, n_pages, ch):
    q = q_ref[0]
    lane = lax.broadcasted_iota(I32, (1, LANE), 1)
    far, last, new = [], [], []
    for h in range(8):
        hh = min(h, A_HEADS - 1)
        far.append(jnp.full((1, LANE), rb_ref[N_BUCKETS - 1, hh], F32))
        last.append(_t5_bias_tile(rb_ref, hh, PAGE_SIZE - lane))
        new.append(jnp.full((1, 1), rb_ref[0, hh], F32))
    far, last, new = (jnp.concatenate(v, axis=0) for v in (far, last, new))

    def compute(c, kv_t, state):
        lg = _lane_tiles(_dot(q, kv_t[:HEAD_DIM]), ch)
        add = [msk_ref[c * ch + k, 0] + jnp.where(c * ch + k == n_pages - 1, last, far) for k in range(ch)]
        lg = jnp.concatenate([t + a for t, a in zip(lg, add)], axis=1)
        return _softmax_step(lg, kv_t, state)

    state = _paged_loop(pt_ref, cache_ref, layer, buf, sem, n_pages // ch, ch, compute, _softmax_init())
    row_new = kvn_ref[0].astype(F32)
    lg_new = (jnp.sum(q.astype(F32) * row_new[:, :HEAD_DIM], axis=1, keepdims=True) + new
              + msk_ref[n_pages, 0][:, 0:1])
    o_ref[0] = _softmax_finish(lg_new, row_new, state)


def _sample_a(page_table, cache, layer, rel_bias, q, kv_new, mask, ch):
    db, n_pages = page_table.shape
    per_b = lambda shape: pl.BlockSpec((1,) + shape, lambda b, pt: (b, 0, 0))
    mask4 = mask.reshape(n_pages + 1, db, 1, LANE)
    return _paged_call(
        functools.partial(_sample_a_kernel, layer=layer, n_pages=n_pages, ch=ch), "sample_a",
        page_table, cache, (rel_bias, q, kv_new, mask4),
        [pl.BlockSpec(memory_space=pltpu.SMEM), per_b((8, HEAD_DIM)), per_b((1, LANE)),
         pl.BlockSpec((n_pages + 1, 1, 1, LANE), lambda b, pt: (0, b, 0, 0))],
        jax.ShapeDtypeStruct((db, 8, LANE), F32), per_b((8, LANE)), ch)


def _split_bf16(x):
    hi = x.astype(BF16)
    return hi, (x - hi.astype(F32)).astype(BF16)


def _sample_b_kernel(pt_ref, q_ref, sfx_ref, cross_ref, cache_ref, o_ref, buf, sem, *, layer, n_pages, ch):
    b, nb = pl.program_id(0), pl.num_programs(0)
    n_chunks = n_pages // ch
    q = q_ref[0]

    def start(seq, c, slot):
        for cp in _page_copies(pt_ref, cache_ref, layer, buf, sem, slot, ch, lambda k: pt_ref[seq, c * ch + k]):
            cp.start()

    def wait(slot):
        for cp in _page_copies(pt_ref, cache_ref, layer, buf, sem, slot, ch, lambda k: 0):
            cp.wait()

    def compute(kv_t, right, acc):
        z = jnp.concatenate(_lane_tiles(_dot(q, kv_t[:2 * HEAD_DIM]), ch), axis=0)
        ls = _log_sigmoid(z)
        hi, lo = _split_bf16(ls - z)
        cum = _dot(hi, sfx_ref[...]) + _dot(lo, sfx_ref[...])
        tot = cum[:, LANE:]
        thi, tlo = _split_bf16(tot)
        off = _dot(cross_ref[...], thi) + _dot(cross_ref[...], tlo)
        w = jnp.exp(ls + cum[:, :LANE] + off + jnp.concatenate([right] * ch, axis=0))
        wl = jnp.concatenate([w[k * 8:(k + 1) * 8] for k in range(ch)], axis=1).astype(BF16)
        return right + off[:8] + tot[:8], acc + _nt_dot(wl, kv_t[2 * HEAD_DIM:])

    @pl.when(b == 0)
    def _():
        start(0, n_chunks - 1, 0)

    @pl.when(b + 1 < nb)
    def _():
        start(b + 1, n_chunks - 1, (b + 1) & 1)

    wait(b & 1)
    zero = jnp.zeros((8, LANE), F32)
    right, acc = compute(_chunk_slab(buf.at[b & 1], ch), zero, zero)

    def more(s):
        c, right, _ = s
        return (c >= 0) & (jnp.max(right) > SURVIVAL_CUTOFF)

    def older(s):
        c, right, acc = s
        start(b, c, 2)
        wait(2)
        right, acc = compute(_chunk_slab(buf.at[2], ch), right, acc)
        return c - 1, right, acc

    _, _, acc = lax.while_loop(more, older, (n_chunks - 2, right, acc))
    o_ref[0] = acc


def _sample_b(page_table, cache, layer, q, sfx, cross, ch):
    db, _ = page_table.shape
    per_b = lambda shape: pl.BlockSpec((1,) + shape, lambda b, pt: (b, 0, 0))
    full2 = lambda a: pl.BlockSpec(a.shape, lambda b, pt: (0, 0))
    return _paged_call(
        functools.partial(_sample_b_kernel, layer=layer, n_pages=page_table.shape[1], ch=ch), "sample_b",
        page_table, cache, (q, sfx, cross), [per_b((8, LANE)), full2(sfx), full2(cross)],
        jax.ShapeDtypeStruct((db, 8, LANE), F32), per_b((8, LANE)), ch, slots=3)


def _sample_c_kernel(pt_ref, q_ref, rown_ref, cache_ref, o_ref, buf, sem, *, layer, n_pages, ch):
    q = q_ref[0]
    width = C_KV_RANK + C_ROPE_DIM

    def compute(c, rows_t, state):
        lg = _dot(q[:, :C_KV_RANK], rows_t[:C_KV_RANK]) + _dot(q[:, C_KV_RANK:width], rows_t[C_KV_RANK:])
        return _softmax_step(lg, rows_t[:C_KV_RANK], state)

    state = _paged_loop(pt_ref, cache_ref, layer, buf, sem, n_pages // ch, ch, compute, _softmax_init())
    row_new = rown_ref[0].astype(F32)
    lg_new = jnp.sum(q.astype(F32) * row_new, axis=1, keepdims=True)
    o_ref[0] = _softmax_finish(lg_new, row_new[:, :C_KV_RANK], state)


def _sample_c(page_table, cache, layer, q, row_new, ch):
    db, _ = page_table.shape
    per_b = lambda shape: pl.BlockSpec((1,) + shape, lambda b, pt: (b, 0, 0))
    return _paged_call(
        functools.partial(_sample_c_kernel, layer=layer, n_pages=page_table.shape[1], ch=ch), "sample_c",
        page_table, cache, (q, row_new), [per_b((8, 2 * LANE)), per_b((1, 2 * LANE))],
        jax.ShapeDtypeStruct((db, 8, LANE), F32), per_b((8, LANE)), ch)


def _c_up_kernel(ol_ref, wuv_ref, o_ref):
    o_ref[...] = jnp.concatenate([_dot(ol_ref[:, h, :].astype(BF16), wuv_ref[h]) for h in range(C_HEADS)], axis=1)


def _c_up(o_lat, wuv):
    db = o_lat.shape[0]
    return pl.pallas_call(
        _c_up_kernel,
        out_shape=jax.ShapeDtypeStruct((db, C_HEADS * HEAD_DIM), F32),
        name="sample_c_up",
    )(o_lat, wuv)


def _cross_page_matrix(ch):
    idx = jnp.arange(ch * 8)
    return ((idx[:, None] % 8 == idx[None, :] % 8) & (idx[None, :] // 8 > idx[:, None] // 8)).astype(BF16)


def _layer_weights(l, w_in, c_q_norm, c_kv_norm, w_uq, w_uk, w_uv, w_out):
    wukt = jnp.transpose(w_uk[l], (1, 2, 0)).astype(BF16)
    wuv = jnp.transpose(w_uv[l], (1, 0, 2)).astype(BF16)
    proj = (_permute_w_in(w_in[l]), c_q_norm[l].reshape(1, -1), c_kv_norm[l].reshape(1, -1),
            _permute_w_uq(w_uq[l]), wukt, _rope_select())
    return proj, wuv, w_out[l].astype(BF16)


def _prompt_layer(hp, t_out, proj_w, wuv, w_out, lg, lb, tabs, rel_bias, pfx, sfx, n_sel, alpha):
    rows = hp.shape[1] // 4
    (aq, akv32, akv16, aiq, aik32, aik16, aiw, gates, bq, bkv32, bkvh, cqf, crow32, crow16) = _project(
        hp, t_out, rows, proj_w, tabs)
    oa = _attn_a(rel_bias, aiq, aiw, aq, aik16, akv16, pfx, n_sel)
    ob = _attn_b(bq, bkvh, sfx)
    oc = _attn_c(cqf, crow16, wuv)
    hp = _merge(hp, gates, oa, ob, oc, w_out, lg, lb, rows, alpha)
    return hp, (akv32, aik32, bkv32, crow32)


def _sample_layer(hs, l, proj_w, wuv, w_out, lg, lb, tabs, rel_bias, pfx, sfx, cross, n_sel, alpha,
                  page_table, cache_a_kv, cache_a_kidx, cache_b_kv, cache_c_kv, ch):
    db = hs.shape[0]
    (aq, akv32, akv16, aiq, aik32, aik16, aiw, gates, bq, bkv32, bkvh, cqf, crow32, crow16) = _project(
        hs[None], db, db, proj_w, tabs)
    by_seq = lambda a: jnp.transpose(a[0], (1, 0, 2))

    scores = _sample_idx(page_table, cache_a_kidx, l, by_seq(aiq), aiw[0, :, IDX_DIM:IDX_DIM + IDX_HEADS, None],
                         aik16[0, :, None, :IDX_DIM], ch)
    mask = _sample_topk(scores, pfx, n_sel)
    qa = jnp.pad(by_seq(aq), ((0, 0), (0, 8 - A_HEADS), (0, 0)))
    oa = _sample_a(page_table, cache_a_kv, l, rel_bias, qa, akv16[0, :, None, :], mask, ch)
    oa = oa[:, :A_HEADS, HEAD_DIM:].reshape(db, A_HEADS * HEAD_DIM)

    qb = by_seq(bq)
    zb = jnp.zeros_like(qb[:, :2])
    qb = jnp.concatenate([jnp.concatenate([qb[:, :2], zb], axis=2), jnp.concatenate([zb, qb[:, 2:]], axis=2),
                          jnp.zeros((db, 8 - B_HEADS, LANE), BF16)], axis=1)
    ob = _sample_b(page_table, cache_b_kv, l, qb, sfx, cross, cross.shape[0] // 8)
    ob = jnp.concatenate([ob[:, 0, :HEAD_DIM], ob[:, 1, :HEAD_DIM], ob[:, 2, HEAD_DIM:], ob[:, 3, HEAD_DIM:]], axis=1)

    o_lat = _sample_c(page_table, cache_c_kv, l, by_seq(cqf), crow16[0, :, None, :], ch)
    oc = _c_up(o_lat, wuv)
    hs = _merge(hs[None], gates, oa[None], ob[None], oc[None], w_out, lg, lb, db, alpha)[0]
    return hs, (akv32[0], aik32[0], bkv32[0], crow32[0])


def kernel(x_prompt, x_sample, cache_a_kv, cache_a_kidx, cache_b_kv, cache_c_kv, page_table, meta_tokens, rel_bias,
           emb_ln_g, emb_ln_b, w_in, c_q_norm, c_kv_norm, w_uq, w_uk, w_uv, w_out, ln_g, ln_b):
    bp, seq, d = x_prompt.shape
    db, dec_seq, _ = x_sample.shape
    assert dec_seq == 1
    depth = w_in.shape[0]
    n_phys = cache_a_kv.shape[0]
    n_pages = page_table.shape[1]
    past = n_pages * PAGE_SIZE
    t_p = N_META + seq
    t_pad = -(-t_p // TILE) * TILE
    n_sel_p = min(TOPK_MAX, seq // 4)
    n_sel_s = min(TOPK_MAX, (past + dec_seq) // 4)
    alpha = (2.0 * depth) ** 0.25
    ch = min(16, n_pages // 2)

    meta = jnp.broadcast_to(meta_tokens[None], (bp, N_META, d))
    xp = jnp.concatenate([meta, x_prompt, jnp.zeros((bp, t_pad - t_p, d), F32)], axis=1).reshape(bp * t_pad, d)
    ln_rows = 4 * TILE if (bp * t_pad) % (4 * TILE) == 0 else TILE
    hp = _layernorm(xp, emb_ln_g, emb_ln_b, ln_rows).reshape(bp, t_pad, d)
    hs = _layernorm(x_sample.reshape(db, d), emb_ln_g, emb_ln_b, db)

    tabs_p = _rope_tables(jnp.arange(t_pad, dtype=I32))
    tabs_s = _rope_tables(jnp.full((db,), past, I32))
    pfx, sfx = _prefix_matrices()
    cross = _cross_page_matrix(min(4, n_pages // 2))
    ca_idx = jnp.transpose(cache_a_kidx, (0, 1, 3, 2))
    ca_kv = jnp.transpose(cache_a_kv, (0, 1, 3, 4, 2)).reshape(n_phys, depth, 2 * HEAD_DIM, PAGE_SIZE)
    cb_kv = jnp.transpose(cache_b_kv, (0, 1, 3, 4, 5, 2)).reshape(n_phys, depth, 4 * HEAD_DIM, PAGE_SIZE)
    cc_kv = jnp.transpose(cache_c_kv, (0, 1, 3, 2))

    rows_p, rows_s = [], []
    for l in range(depth):
        proj_w, wuv, wo = _layer_weights(l, w_in, c_q_norm, c_kv_norm, w_uq, w_uk, w_uv, w_out)
        hp, rp = _prompt_layer(hp, t_p, proj_w, wuv, wo, ln_g[l], ln_b[l], tabs_p, rel_bias, pfx, sfx, n_sel_p, alpha)
        hs, rs = _sample_layer(hs, l, proj_w, wuv, wo, ln_g[l], ln_b[l], tabs_s, rel_bias, pfx, sfx, cross, n_sel_s,
                               alpha, page_table, ca_kv, ca_idx, cb_kv, cc_kv, ch)
        rows_p.append(rp)
        rows_s.append(rs)

    def stack(rows, k, lead, tail):
        return jnp.stack([r[k].reshape(lead + tail) for r in rows], axis=1)

    lp, ls = (bp, t_p), (db, dec_seq)
    return (hp[:, N_META:t_p], hs[:, None, :],
            stack(rows_p, 0, lp, (2, HEAD_DIM)), stack(rows_p, 1, lp, (IDX_DIM,)),
            stack(rows_p, 2, lp, (2, B_KV_HEADS, HEAD_DIM)), stack(rows_p, 3, lp, (C_KV_RANK + C_ROPE_DIM,)),
            stack(rows_s, 0, ls, (2, HEAD_DIM)), stack(rows_s, 1, ls, (IDX_DIM,)),
            stack(rows_s, 2, ls, (2, B_KV_HEADS, HEAD_DIM)), stack(rows_s, 3, ls, (C_KV_RANK + C_ROPE_DIM,)))
```

```python
import functools
import math

import jax
import jax.numpy as jnp
from jax import lax
from jax.experimental import pallas as pl
from jax.experimental.pallas import tpu as pltpu

F32, BF16, I32 = jnp.float32, jnp.bfloat16, jnp.int32

N_META = 16
HEAD_DIM = 64
A_HEADS = 4
IDX_HEADS = 8
IDX_DIM = 64
TOPK_MAX = 256
B_HEADS = 4
B_KV_HEADS = 2
C_HEADS = 8
C_NOPE_DIM = 64
C_ROPE_DIM = 32
C_KV_RANK = 128
N_BUCKETS = 32
MAX_DISTANCE = 128
ROPE_THETA = 10000.0
PAGE_SIZE = 128
LN_EPS = 1e-5
RMS_EPS = 1e-6

LANE = 128
TILE = 128
VMEM_LIMIT = 56 * 1024 * 1024
NEG = -1e30
SURVIVAL_CUTOFF = -100.0
PREFETCH_DEPTH = 3
PAGED_SLOTS = PREFETCH_DEPTH + 1
INT_MIN = -2 ** 31
C_SCALE = (C_NOPE_DIM + C_ROPE_DIM) ** -0.5
IW_SCALE = IDX_HEADS ** -0.5 * IDX_DIM ** -0.5

SEG_AQ = (0, 256)
SEG_AKV = (256, 384)
SEG_AIQ = (384, 896)
SEG_AIK = (896, 1024)
SEG_G = (1024, 2048)
SEG_BQ = (2048, 2304)
SEG_BKV = (2304, 2560)
SEG_CQ = (2560, 2816)
SEG_CKV = (2816, 2944)
SEG_CKR = (2944, 3072)
IN_PERM_WIDTH = 3072


def _params(*sem):
    return pltpu.CompilerParams(dimension_semantics=sem, vmem_limit_bytes=VMEM_LIMIT)


def _nt_dot(a, b):
    return lax.dot_general(a, b, (((1,), (1,)), ((), ())), preferred_element_type=F32)


def _dot(a, b):
    return jnp.dot(a, b, preferred_element_type=F32)


def _ln_kernel(x_ref, g_ref, b_ref, o_ref):
    x = x_ref[...]
    mu = jnp.mean(x, -1, keepdims=True)
    xc = x - mu
    var = jnp.mean(xc * xc, -1, keepdims=True)
    o_ref[...] = xc * lax.rsqrt(var + LN_EPS) * g_ref[...] + b_ref[...]


def _layernorm(x2d, g, b, rows):
    n, d = x2d.shape
    return pl.pallas_call(
        _ln_kernel,
        out_shape=jax.ShapeDtypeStruct((n, d), F32),
        grid=(n // rows,),
        in_specs=[pl.BlockSpec((rows, d), lambda i: (i, 0)),
                  pl.BlockSpec((1, d), lambda i: (0, 0)),
                  pl.BlockSpec((1, d), lambda i: (0, 0))],
        out_specs=pl.BlockSpec((rows, d), lambda i: (i, 0)),
        compiler_params=_params("arbitrary"),
        name="embed_ln",
    )(x2d, g.reshape(1, d), b.reshape(1, d))


def _project_kernel(x_ref, w_ref, qn_ref, kvn_ref, wuq_ref, wukt_ref, sel_ref,
                    cq_ref, sq_ref, ck_ref, sa_ref, sb_ref,
                    aq_ref, akv32_ref, akv16_ref, aiq_ref, aik32_ref, aik16_ref, aiw_ref, g_ref,
                    bq_ref, bkv32_ref, bkvh_ref, cqf_ref, crow32_ref, crow16_ref):
    xb = x_ref[0].astype(BF16)

    def seg(ab):
        return _dot(xb, w_ref[:, ab[0]:ab[1]])

    s = seg(SEG_AQ)
    for h in range(A_HEADS):
        aq_ref[0, h] = (s[:, h * 64:(h + 1) * 64] * HEAD_DIM ** -0.5).astype(BF16)
    s = seg(SEG_AKV)
    akv32_ref[0] = s
    akv16_ref[0] = s.astype(BF16)
    s = seg(SEG_AIQ)
    for h in range(IDX_HEADS):
        aiq_ref[0, h] = s[:, h * 64:(h + 1) * 64].astype(BF16)
    s = seg(SEG_AIK)
    aik32_ref[0] = s[:, :IDX_DIM]
    aik16_ref[0] = s.astype(BF16)
    aiw_ref[0] = s * IW_SCALE
    g_ref[0] = seg(SEG_G)
    s = seg(SEG_BQ)
    for h in range(B_HEADS):
        bq_ref[0, h] = (s[:, h * 64:(h + 1) * 64] * HEAD_DIM ** -0.5).astype(BF16)
    s = seg(SEG_BKV)
    bkv32_ref[0] = s
    for h in range(2 * B_KV_HEADS):
        bkvh_ref[0, h] = s[:, h * 64:(h + 1) * 64].astype(BF16)

    cq = seg(SEG_CQ)
    cqn = cq * lax.rsqrt(jnp.mean(cq * cq, -1, keepdims=True) + RMS_EPS) * qn_ref[...]
    cu = _dot(cqn.astype(BF16), wuq_ref[...])
    r1, r2 = cu[:, 512:640], cu[:, 640:768]
    c, sn = cq_ref[...], sq_ref[...]
    rot = (jnp.concatenate([r1 * c - r2 * sn, r1 * sn + r2 * c], axis=1) * C_SCALE).astype(BF16)
    for h in range(C_HEADS):
        qlat = _dot(cu[:, h * 64:(h + 1) * 64].astype(BF16), wukt_ref[h]) * C_SCALE
        qrope = _dot(rot, sel_ref[h])
        cqf_ref[0, h] = jnp.concatenate([qlat, qrope], axis=1).astype(BF16)

    ckv = seg(SEG_CKV)
    ckvn = ckv * lax.rsqrt(jnp.mean(ckv * ckv, -1, keepdims=True) + RMS_EPS) * kvn_ref[...]
    kr = seg(SEG_CKR)
    krot = kr * ck_ref[...] + pltpu.roll(kr, LANE - 16, 1) * sa_ref[...] + pltpu.roll(kr, 16, 1) * sb_ref[...]
    crow32_ref[0, :, 0:C_KV_RANK] = ckvn
    crow32_ref[0, :, C_KV_RANK:C_KV_RANK + C_ROPE_DIM] = krot[:, :C_ROPE_DIM]
    crow16_ref[0] = jnp.concatenate([ckvn, krot], axis=1).astype(BF16)


def _project(x, t_out, rows, wts, tabs):
    bn, t, d = x.shape
    nt = t // rows
    w_in, qn, kvn, wuq, wukt, sel = wts

    def full(a):
        nd = a.ndim
        return pl.BlockSpec(a.shape, lambda b, i, _n=nd: (0,) * _n)

    def rows3(w):
        return pl.BlockSpec((1, rows, w), lambda b, i: (b, i, 0))

    def heads4(nh, w):
        return pl.BlockSpec((1, nh, rows, w), lambda b, i: (b, 0, i, 0))

    tab_spec = pl.BlockSpec((rows, LANE), lambda b, i: (i, 0))
    out_shape = [
        jax.ShapeDtypeStruct((bn, A_HEADS, t, 64), BF16),
        jax.ShapeDtypeStruct((bn, t_out, 128), F32),
        jax.ShapeDtypeStruct((bn, t, 128), BF16),
        jax.ShapeDtypeStruct((bn, IDX_HEADS, t, 64), BF16),
        jax.ShapeDtypeStruct((bn, t_out, 64), F32),
        jax.ShapeDtypeStruct((bn, t, 128), BF16),
        jax.ShapeDtypeStruct((bn, t, 128), F32),
        jax.ShapeDtypeStruct((bn, t, 1024), F32),
        jax.ShapeDtypeStruct((bn, B_HEADS, t, 64), BF16),
        jax.ShapeDtypeStruct((bn, t_out, 256), F32),
        jax.ShapeDtypeStruct((bn, 4, t, 64), BF16),
        jax.ShapeDtypeStruct((bn, C_HEADS, t, 256), BF16),
        jax.ShapeDtypeStruct((bn, t_out, 160), F32),
        jax.ShapeDtypeStruct((bn, t, 256), BF16),
    ]
    out_specs = [heads4(A_HEADS, 64), rows3(128), rows3(128), heads4(IDX_HEADS, 64), rows3(64), rows3(128),
                 rows3(128), rows3(1024), heads4(B_HEADS, 64), rows3(256), heads4(4, 64),
                 heads4(C_HEADS, 256), rows3(160), rows3(256)]
    return pl.pallas_call(
        _project_kernel,
        out_shape=out_shape,
        grid=(bn, nt),
        in_specs=[rows3(d), full(w_in), full(qn), full(kvn), full(wuq), full(wukt), full(sel)] + [tab_spec] * 5,
        out_specs=out_specs,
        compiler_params=_params("arbitrary", "arbitrary"),
        name="project",
    )(x, w_in, qn, kvn, wuq, wukt, sel, *tabs)


def _merge_kernel(x_ref, g_ref, oa_ref, ob_ref, oc_ref, w_ref, lg_ref, lb_ref, y_ref, *, alpha):
    g = g_ref[0]
    o = jnp.concatenate([oa_ref[0], ob_ref[0], oc_ref[0]], axis=1)
    mixed = (o * (g * jax.nn.sigmoid(g))).astype(BF16)
    z = alpha * x_ref[0] + _dot(mixed, w_ref[...])
    mu = jnp.mean(z, -1, keepdims=True)
    zc = z - mu
    var = jnp.mean(zc * zc, -1, keepdims=True)
    y_ref[0] = zc * lax.rsqrt(var + LN_EPS) * lg_ref[...] + lb_ref[...]


def _merge(x, gates, oa, ob, oc, w_out, lg, lb, rows, alpha):
    bn, t, d = x.shape

    def rows3(w):
        return pl.BlockSpec((1, rows, w), lambda b, i: (b, i, 0))

    def full(a):
        nd = a.ndim
        return pl.BlockSpec(a.shape, lambda b, i, _n=nd: (0,) * _n)

    lg2, lb2 = lg.reshape(1, d), lb.reshape(1, d)
    return pl.pallas_call(
        functools.partial(_merge_kernel, alpha=alpha),
        out_shape=jax.ShapeDtypeStruct((bn, t, d), F32),
        grid=(bn, t // rows),
        in_specs=[rows3(d), rows3(1024), rows3(256), rows3(256), rows3(512), full(w_out), full(lg2), full(lb2)],
        out_specs=rows3(d),
        compiler_params=_params("arbitrary", "arbitrary"),
        name="merge",
    )(x, gates, oa, ob, oc, w_out, lg2, lb2)


def _t5_bias_tile(rb_ref, h, dist):
    max_exact = N_BUCKETS // 2
    lg = jnp.log(jnp.maximum(dist, 1).astype(F32) / max_exact) / math.log(MAX_DISTANCE / max_exact)
    large = jnp.minimum(max_exact + (lg * (N_BUCKETS - max_exact)).astype(I32), N_BUCKETS - 1)
    bucket = jnp.where(dist < max_exact, dist, large)
    out = jnp.zeros(dist.shape, F32)
    for k in range(N_BUCKETS):
        out = jnp.where(bucket == k, rb_ref[k, h], out)
    return out


def _sortable(score):
    bits = pltpu.bitcast(score + 0.0, I32)
    return bits ^ ((bits >> 31) & 0x7FFFFFFF)


def _kth_largest(count_ge, shape, k):
    def body(t, tau):
        bit = lax.shift_left(jnp.int32(1), 31 - t)
        cand = tau + bit
        return jnp.where(count_ge(cand) >= k, cand, tau)

    return lax.fori_loop(0, 32, body, jnp.full(shape, INT_MIN, I32))


def _attn_a_kernel(rb_ref, iq_ref, iw_ref, q_ref, ik_ref, kv_ref, pfx_ref, o_ref,
                   bias_scr, wb_scr, key_scr, keyt_scr, msk_scr, lg_scr, mx_scr, acc_scr, ls_scr, *, n_sel):
    b, i = pl.program_id(0), pl.program_id(1)
    nkt = i + 1
    row = lax.broadcasted_iota(I32, (TILE, TILE), 0)
    col = lax.broadcasted_iota(I32, (TILE, TILE), 1)

    @pl.when((b == 0) & (i == 0))
    def _():
        for h in range(A_HEADS):
            bias_scr[h, 0] = _t5_bias_tile(rb_ref, h, jnp.maximum(row - col, 0))
            bias_scr[h, 1] = _t5_bias_tile(rb_ref, h, TILE + row - col)
            bias_scr[h, 2] = _t5_bias_tile(rb_ref, h, jnp.full((TILE, TILE), 2 * TILE, I32))

    iw = iw_ref[0]
    for h in range(IDX_HEADS):
        wb_scr[h] = jnp.broadcast_to(iw[:, IDX_DIM + h:IDX_DIM + h + 1], (TILE, TILE))
    iqs = iq_ref[0].reshape(IDX_HEADS * TILE, IDX_DIM)

    def score_keys(start, width):
        kt = ik_ref[0, pl.ds(pl.multiple_of(start, TILE), width), :][:, :IDX_DIM]
        s = _nt_dot(iqs, kt)
        acc = jnp.zeros((TILE, width), F32)
        for h in range(IDX_HEADS):
            wb = wb_scr[h] if width == TILE else jnp.concatenate([wb_scr[h]] * (width // TILE), axis=1)
            acc = acc + wb * jnp.maximum(s[h * TILE:(h + 1) * TILE], 0.0)
        return _sortable(acc)

    def score_pair(jj, carry):
        keys = score_keys(jj * 2 * TILE, 2 * TILE)
        key_scr[2 * jj] = keys[:, :TILE]
        key_scr[2 * jj + 1] = keys[:, TILE:]
        return carry

    def score_tile(j, carry):
        key_scr[j] = score_keys(j * TILE, TILE)
        return carry

    lax.fori_loop(0, nkt // 2, score_pair, 0)
    lax.fori_loop(2 * (nkt // 2), nkt, score_tile, 0)
    key_scr[i] = jnp.where(col <= row, key_scr[i], INT_MIN)

    def transpose_tile(j, carry):
        keyt_scr[j] = key_scr[j].T
        return carry

    lax.fori_loop(0, nkt, transpose_tile, 0)

    def count(pred):
        def tile_count(j):
            return jnp.sum(jnp.where(pred(keyt_scr[j]), 1, 0).reshape(TILE // 8, 8, TILE), axis=0)

        c = lax.fori_loop(0, nkt // 2, lambda jj, c: c + tile_count(2 * jj) + tile_count(2 * jj + 1),
                          jnp.zeros((8, TILE), I32))
        c = c + jnp.where(nkt % 2 == 1, tile_count(nkt - 1), 0)
        return jnp.sum(c, axis=0, keepdims=True)

    tau_t = _kth_largest(lambda cand: count(lambda kt: kt >= cand), (1, TILE), n_sel)
    tau_t = jnp.maximum(tau_t, INT_MIN + 1)
    take_t = (n_sel - count(lambda kt: kt > tau_t)).astype(F32)
    tau_b = jnp.broadcast_to(tau_t, (TILE, TILE)).T
    take_b = jnp.broadcast_to(take_t, (TILE, TILE)).T

    n_eq_t = count(lambda kt: kt == tau_t).astype(F32)
    split_ties = jnp.max(jnp.where(n_eq_t > take_t, 1, 0)) > 0

    def mask_tile_ties(j, seen):
        kt = key_scr[j]
        eq = kt == tau_b
        both = _dot(jnp.where(eq, 1.0, 0.0).astype(BF16), pfx_ref[...])
        sel = (kt > tau_b) | (eq & (both[:, :TILE] + seen < take_b))
        msk_scr[j] = jnp.where(sel, 0.0, NEG)
        return seen + both[:, TILE:]

    def mask_tile(j, carry):
        msk_scr[j] = jnp.where(key_scr[j] >= tau_b, 0.0, NEG)
        return carry

    @pl.when(split_ties)
    def _():
        lax.fori_loop(0, nkt, mask_tile_ties, jnp.zeros((TILE, TILE), F32))

    @pl.when(jnp.logical_not(split_ties))
    def _():
        lax.fori_loop(0, nkt, mask_tile, 0)

    qs = q_ref[0].reshape(A_HEADS * TILE, HEAD_DIM)
    mx_scr[...] = jnp.full(mx_scr.shape, NEG, F32)

    def logits_tile(j, carry):
        kvt = kv_ref[0, pl.ds(pl.multiple_of(j * TILE, TILE), TILE), :]
        lg = _nt_dot(qs, kvt[:, :HEAD_DIM])
        bidx = jnp.minimum(i - j, 2)
        madd = msk_scr[j]
        for h in range(A_HEADS):
            sl = slice(h * TILE, (h + 1) * TILE)
            lgh = lg[sl] + bias_scr[h, bidx] + madd
            lg_scr[j, sl] = lgh
            mx_scr[sl] = jnp.maximum(mx_scr[sl], lgh)
        return carry

    lax.fori_loop(0, nkt, logits_tile, 0)
    m_b = jnp.broadcast_to(jnp.max(mx_scr[...], axis=1, keepdims=True), mx_scr.shape)
    acc_scr[...] = jnp.zeros(acc_scr.shape, F32)
    ls_scr[...] = jnp.zeros(ls_scr.shape, F32)

    def pv_tile(j, carry):
        kvt = kv_ref[0, pl.ds(pl.multiple_of(j * TILE, TILE), TILE), :]
        p = jnp.exp(lg_scr[j] - m_b)
        ls_scr[...] += p
        acc_scr[...] += _dot(p.astype(BF16), kvt)
        return carry

    lax.fori_loop(0, nkt, pv_tile, 0)
    o = acc_scr[...] / jnp.sum(ls_scr[...], axis=1, keepdims=True)
    o_ref[0] = jnp.concatenate([o[h * TILE:(h + 1) * TILE, HEAD_DIM:] for h in range(A_HEADS)], axis=1)


def _attn_a(rel_bias, aiq, aiw, aq, aik16, akv16, pfx, n_sel):
    bn, _, t, _ = aq.shape
    nt = t // TILE
    return pl.pallas_call(
        functools.partial(_attn_a_kernel, n_sel=n_sel),
        out_shape=jax.ShapeDtypeStruct((bn, t, A_HEADS * HEAD_DIM), F32),
        grid=(bn, nt),
        in_specs=[pl.BlockSpec(memory_space=pltpu.SMEM),
                  pl.BlockSpec((1, IDX_HEADS, TILE, 64), lambda b, i: (b, 0, i, 0)),
                  pl.BlockSpec((1, TILE, 128), lambda b, i: (b, i, 0)),
                  pl.BlockSpec((1, A_HEADS, TILE, 64), lambda b, i: (b, 0, i, 0)),
                  pl.BlockSpec((1, t, 128), lambda b, i: (b, 0, 0)),
                  pl.BlockSpec((1, t, 128), lambda b, i: (b, 0, 0)),
                  pl.BlockSpec((TILE, 2 * TILE), lambda b, i: (0, 0))],
        out_specs=pl.BlockSpec((1, TILE, A_HEADS * HEAD_DIM), lambda b, i: (b, i, 0)),
        scratch_shapes=[pltpu.VMEM((A_HEADS, 3, TILE, TILE), F32),
                        pltpu.VMEM((IDX_HEADS, TILE, TILE), F32),
                        pltpu.VMEM((nt, TILE, TILE), I32),
                        pltpu.VMEM((nt, TILE, TILE), I32),
                        pltpu.VMEM((nt, TILE, TILE), F32),
                        pltpu.VMEM((nt, A_HEADS * TILE, TILE), F32),
                        pltpu.VMEM((A_HEADS * TILE, TILE), F32),
                        pltpu.VMEM((A_HEADS * TILE, TILE), F32),
                        pltpu.VMEM((A_HEADS * TILE, TILE), F32)],
        compiler_params=_params("arbitrary", "arbitrary"),
        name="attn_a_prompt",
    )(rel_bias, aiq, aiw, aq, aik16, akv16, pfx)


def _log_sigmoid(z):
    return jnp.minimum(z, 0.0) - jnp.log1p(jnp.exp(-jnp.abs(z)))


def _attn_b_kernel(q_ref, kv_ref, sfx_ref, o_ref, right_scr, acc_scr):
    i = pl.program_id(1)
    rows = (B_HEADS // B_KV_HEADS) * TILE
    row = lax.broadcasted_iota(I32, (rows, TILE), 0) & (TILE - 1)
    col = lax.broadcasted_iota(I32, (rows, TILE), 1)
    qs = [q_ref[0, 2 * g:2 * g + 2].reshape(rows, HEAD_DIM) for g in range(B_KV_HEADS)]

    def tile(j, diagonal):
        ks = pl.ds(pl.multiple_of(j * TILE, TILE), TILE)
        for g in range(B_KV_HEADS):
            z = _nt_dot(qs[g], kv_ref[0, g, ks, :])
            ls = _log_sigmoid(z)
            lneg = ls - z
            if diagonal:
                lneg = jnp.where(col < row, lneg, 0.0)
            hi, lo = _split_bf16(lneg)
            cum = _dot(hi, sfx_ref[...]) + _dot(lo, sfx_ref[...])
            w = jnp.exp(ls + cum[:, :TILE] + right_scr[g])
            if diagonal:
                w = jnp.where(col < row, w, 0.0)
            acc_scr[g] += _dot(w.astype(BF16), kv_ref[0, B_KV_HEADS + g, ks, :])
            right_scr[g] += cum[:, TILE:]

    right_scr[...] = jnp.zeros(right_scr.shape, F32)
    acc_scr[...] = jnp.zeros(acc_scr.shape, F32)
    tile(i, True)

    def more(j):
        return (j >= 0) & (jnp.max(right_scr[...]) > SURVIVAL_CUTOFF)

    def older(j):
        tile(j, False)
        return j - 1

    lax.while_loop(more, older, i - 1)
    o_ref[0] = jnp.concatenate([acc_scr[g, r * TILE:(r + 1) * TILE] for g in range(B_KV_HEADS) for r in range(2)],
                               axis=1)


def _attn_b(bq, bkvh, sfx):
    bn, _, t, _ = bq.shape
    return pl.pallas_call(
        _attn_b_kernel,
        out_shape=jax.ShapeDtypeStruct((bn, t, B_HEADS * HEAD_DIM), F32),
        grid=(bn, t // TILE),
        in_specs=[pl.BlockSpec((1, B_HEADS, TILE, 64), lambda b, i: (b, 0, i, 0)),
                  pl.BlockSpec((1, 4, t, 64), lambda b, i: (b, 0, 0, 0)),
                  pl.BlockSpec((TILE, 2 * TILE), lambda b, i: (0, 0))],
        out_specs=pl.BlockSpec((1, TILE, B_HEADS * HEAD_DIM), lambda b, i: (b, i, 0)),
        scratch_shapes=[pltpu.VMEM((B_KV_HEADS, 2 * TILE, TILE), F32),
                        pltpu.VMEM((B_KV_HEADS, 2 * TILE, HEAD_DIM), F32)],
        compiler_params=_params("arbitrary", "arbitrary"),
        name="attn_b_prompt",
    )(bq, bkvh, sfx)


def _attn_c_kernel(q_ref, c_ref, wuv_ref, o_ref, lg_scr, mx_scr, acc_scr, ls_scr):
    i = pl.program_id(1)
    rows = C_HEADS * TILE
    row = lax.broadcasted_iota(I32, (rows, TILE), 0) & (TILE - 1)
    col = lax.broadcasted_iota(I32, (rows, TILE), 1)
    qs = q_ref[0].reshape(rows, 2 * LANE)
    mx_scr[...] = jnp.full(mx_scr.shape, NEG, F32)

    n_pairs = i // 2

    def logits_pair(jj, carry):
        ct = c_ref[0, pl.ds(pl.multiple_of(jj * 2 * TILE, 2 * TILE), 2 * TILE), :]
        lg = _nt_dot(qs, ct)
        lg_scr[2 * jj] = lg[:, :TILE]
        lg_scr[2 * jj + 1] = lg[:, TILE:]
        mx_scr[...] = jnp.maximum(mx_scr[...], jnp.maximum(lg[:, :TILE], lg[:, TILE:]))
        return carry

    def logits_tile(j, carry):
        ct = c_ref[0, pl.ds(pl.multiple_of(j * TILE, TILE), TILE), :]
        lg = jnp.where((col <= row) | (j < i), _nt_dot(qs, ct), NEG)
        lg_scr[j] = lg
        mx_scr[...] = jnp.maximum(mx_scr[...], lg)
        return carry

    lax.fori_loop(0, n_pairs, logits_pair, 0)
    lax.fori_loop(2 * n_pairs, i + 1, logits_tile, 0)
    m_b = jnp.broadcast_to(jnp.max(mx_scr[...], axis=1, keepdims=True), (rows, TILE))
    acc_scr[...] = jnp.zeros(acc_scr.shape, F32)
    ls_scr[...] = jnp.zeros(ls_scr.shape, F32)

    def pv_pair(jj, carry):
        ct = c_ref[0, pl.ds(pl.multiple_of(jj * 2 * TILE, 2 * TILE), 2 * TILE), :]
        p0 = jnp.exp(lg_scr[2 * jj] - m_b)
        p1 = jnp.exp(lg_scr[2 * jj + 1] - m_b)
        ls_scr[...] += p0 + p1
        acc_scr[...] += _dot(jnp.concatenate([p0, p1], axis=1).astype(BF16), ct[:, :C_KV_RANK])
        return carry

    def pv_tile(j, carry):
        ct = c_ref[0, pl.ds(pl.multiple_of(j * TILE, TILE), TILE), :]
        p = jnp.exp(lg_scr[j] - m_b)
        ls_scr[...] += p
        acc_scr[...] += _dot(p.astype(BF16), ct[:, :C_KV_RANK])
        return carry

    lax.fori_loop(0, n_pairs, pv_pair, 0)
    lax.fori_loop(2 * n_pairs, i + 1, pv_tile, 0)
    ol = (acc_scr[...] / jnp.sum(ls_scr[...], axis=1, keepdims=True)).astype(BF16)
    o_ref[0] = jnp.concatenate([_dot(ol[h * TILE:(h + 1) * TILE], wuv_ref[h]) for h in range(C_HEADS)], axis=1)


def _attn_c(cqf, crow16, wuv):
    bn, _, t, _ = cqf.shape
    nt = t // TILE
    rows = C_HEADS * TILE
    return pl.pallas_call(
        _attn_c_kernel,
        out_shape=jax.ShapeDtypeStruct((bn, t, C_HEADS * HEAD_DIM), F32),
        grid=(bn, nt),
        in_specs=[pl.BlockSpec((1, C_HEADS, TILE, 256), lambda b, i: (b, 0, i, 0)),
                  pl.BlockSpec((1, t, 256), lambda b, i: (b, 0, 0)),
                  pl.BlockSpec(wuv.shape, lambda b, i: (0, 0, 0))],
        out_specs=pl.BlockSpec((1, TILE, C_HEADS * HEAD_DIM), lambda b, i: (b, i, 0)),
        scratch_shapes=[pltpu.VMEM((nt, rows, TILE), F32),
                        pltpu.VMEM((rows, TILE), F32),
                        pltpu.VMEM((rows, TILE), F32),
                        pltpu.VMEM((rows, TILE), F32)],
        compiler_params=_params("arbitrary", "arbitrary"),
        name="attn_c_prompt",
    )(cqf, crow16, wuv)


def _permute_w_in(w):
    d = w.shape[0]
    names = dict(a_q=(0, 256), a_k=(256, 320), a_v=(320, 384), a_iq=(384, 896), a_ik=(896, 960), a_iw=(960, 968),
                 a_g=(968, 1224), b_q=(1224, 1480), b_k=(1480, 1608), b_v=(1608, 1736), b_g=(1736, 1992),
                 c_q=(1992, 2248), c_kv=(2248, 2376), c_kr=(2376, 2408), c_g=(2408, 2920))

    def c(n):
        return w[:, names[n][0]:names[n][1]]

    z = lambda n: jnp.zeros((d, n), w.dtype)
    out = jnp.concatenate([c("a_q"), c("a_k"), c("a_v"), c("a_iq"), c("a_ik"), c("a_iw"), z(56),
                           c("a_g"), c("b_g"), c("c_g"), c("b_q"), c("b_k"), c("b_v"),
                           c("c_q"), c("c_kv"), c("c_kr"), z(96)], axis=1)
    assert out.shape[1] == IN_PERM_WIDTH
    return out.astype(BF16)


def _permute_w_uq(w):
    r = w.reshape(w.shape[0], C_HEADS, C_NOPE_DIM + C_ROPE_DIM)
    half = C_ROPE_DIM // 2
    nope = r[:, :, :C_NOPE_DIM].reshape(w.shape[0], -1)
    lo = r[:, :, C_NOPE_DIM:C_NOPE_DIM + half].reshape(w.shape[0], -1)
    hi = r[:, :, C_NOPE_DIM + half:].reshape(w.shape[0], -1)
    return jnp.concatenate([nope, lo, hi], axis=1).astype(BF16)


def _rope_select():
    half = C_ROPE_DIM // 2
    src = jnp.arange(2 * LANE)[:, None]
    dst = jnp.arange(LANE)[None, :]
    mats = []
    for h in range(C_HEADS):
        lo = (src == h * half + dst) & (dst < half)
        hi = (src == LANE + h * half + dst - half) & (dst >= half) & (dst < 2 * half)
        mats.append(lo | hi)
    return jnp.stack(mats).astype(BF16)


def _rope_tables(pos):
    half = C_ROPE_DIM // 2
    freq = ROPE_THETA ** (-jnp.arange(half, dtype=F32) / half)
    ang = pos.astype(F32)[:, None] * freq
    c, s = jnp.cos(ang), jnp.sin(ang)
    cq, sq = jnp.tile(c, (1, LANE // half)), jnp.tile(s, (1, LANE // half))
    z = jnp.zeros((pos.shape[0], LANE - 2 * half), F32)
    zh = jnp.zeros_like(c)
    ck = jnp.concatenate([c, c, z], axis=1)
    sa = jnp.concatenate([-s, zh, z], axis=1)
    sb = jnp.concatenate([zh, s, z], axis=1)
    return cq, sq, ck, sa, sb


def _prefix_matrices():
    a = jnp.arange(TILE)
    ones = jnp.ones((TILE, TILE), F32)
    pfx = jnp.concatenate([(a[:, None] < a[None, :]).astype(F32), ones], axis=1).astype(BF16)
    sfx = jnp.concatenate([(a[:, None] > a[None, :]).astype(F32), ones], axis=1).astype(BF16)
    return pfx, sfx


def _page_copies(pt_ref, cache_ref, layer, buf, sem, slot, ch, page_of):
    return [pltpu.make_async_copy(cache_ref.at[page_of(k), layer], buf.at[slot, k], sem.at[slot]) for k in range(ch)]


def _paged_loop(pt_ref, cache_ref, layer, buf, sem, n_chunks, ch, compute, init):
    b, nb = pl.program_id(0), pl.num_programs(0)

    def start(g):
        seq, c = g // n_chunks, g % n_chunks
        for cp in _page_copies(pt_ref, cache_ref, layer, buf, sem, g % PAGED_SLOTS, ch,
                               lambda k: pt_ref[seq, c * ch + k]):
            cp.start()

    def wait(slot):
        for cp in _page_copies(pt_ref, cache_ref, layer, buf, sem, slot, ch, lambda k: 0):
            cp.wait()

    @pl.when(b == 0)
    def _():
        for g in range(PREFETCH_DEPTH):
            start(g)

    def body(c, carry):
        g = b * n_chunks + c

        @pl.when(g + PREFETCH_DEPTH < nb * n_chunks)
        def _():
            start(g + PREFETCH_DEPTH)

        slot = g % PAGED_SLOTS
        wait(slot)
        return compute(c, _chunk_slab(buf.at[slot], ch), carry)

    return lax.fori_loop(0, n_chunks, body, init)


def _chunk_slab(view, ch):
    return jnp.concatenate([view[k] for k in range(ch)], axis=1).astype(BF16)


def _lane_tiles(x, n):
    return [x[:, k * LANE:(k + 1) * LANE] for k in range(n)]


def _paged_call(kernel, name, page_table, layer_cache, operands, operand_specs, out_shape, out_spec, ch,
                slots=PAGED_SLOTS):
    db = page_table.shape[0]
    feat = layer_cache.shape[2]
    grid_spec = pltpu.PrefetchScalarGridSpec(
        num_scalar_prefetch=1,
        grid=(db,),
        in_specs=list(operand_specs) + [pl.BlockSpec(memory_space=pl.ANY)],
        out_specs=out_spec,
        scratch_shapes=[pltpu.VMEM((slots, ch, feat, PAGE_SIZE), F32), pltpu.SemaphoreType.DMA((slots,))],
    )
    return pl.pallas_call(kernel, out_shape=out_shape, grid_spec=grid_spec,
                          compiler_params=_params("arbitrary"), name=name)(page_table, *operands, layer_cache)


def _sample_idx_kernel(pt_ref, iq_ref, w_ref, ikn_ref, cache_ref, out_ref, buf, sem, *, layer, n_pages, ch):
    iq = iq_ref[0]
    w = w_ref[0]

    def compute(c, keys_t, carry):
        sc = jnp.sum(jnp.maximum(_dot(iq, keys_t), 0.0) * w, axis=0, keepdims=True)
        for k, tile in enumerate(_lane_tiles(sc, ch)):
            out_ref[c * ch + k, 0] = tile
        return carry

    _paged_loop(pt_ref, cache_ref, layer, buf, sem, n_pages // ch, ch, compute, 0)
    s_new = jnp.sum(iq.astype(F32) * ikn_ref[0].astype(F32), axis=1, keepdims=True)
    s_new = jnp.sum(jnp.maximum(s_new, 0.0) * w, axis=0, keepdims=True)
    lane = lax.broadcasted_iota(I32, (1, LANE), 1)
    out_ref[n_pages, 0] = jnp.where(lane == 0, s_new, -jnp.inf)


def _sample_idx(page_table, cache, layer, iq, w, ik_new, ch):
    db, n_pages = page_table.shape
    per_b = lambda shape: pl.BlockSpec((1,) + shape, lambda b, pt: (b, 0, 0))
    out = _paged_call(
        functools.partial(_sample_idx_kernel, layer=layer, n_pages=n_pages, ch=ch), "sample_idx",
        page_table, cache, (iq, w, ik_new), [per_b((IDX_HEADS, IDX_DIM)), per_b((IDX_HEADS, 1)), per_b((1, IDX_DIM))],
        jax.ShapeDtypeStruct((n_pages + 1, db, 1, LANE), F32),
        pl.BlockSpec((n_pages + 1, 1, 1, LANE), lambda b, pt: (0, b, 0, 0)), ch)
    return out.reshape(n_pages + 1, db, LANE)


def _sample_topk_kernel(s_ref, pfx_ref, m_ref, key_scr, *, n_sel):
    nt, bb, _ = s_ref.shape

    def to_key(j, carry):
        key_scr[j] = _sortable(s_ref[j])
        return carry

    lax.fori_loop(0, nt, to_key, 0)

    def count(pred):
        c = lax.fori_loop(0, nt, lambda j, c: c + jnp.where(pred(key_scr[j]), 1, 0), jnp.zeros((bb, LANE), I32))
        return jnp.sum(c, axis=1, keepdims=True)

    def count_ge(cand):
        cb = jnp.broadcast_to(cand, (bb, LANE))
        return count(lambda kt: kt >= cb)

    tau = _kth_largest(count_ge, (bb, 1), n_sel)
    tau_b = jnp.broadcast_to(jnp.maximum(tau, INT_MIN + 1), (bb, LANE))
    take_b = jnp.broadcast_to((n_sel - count(lambda kt: kt > tau_b)).astype(F32), (bb, LANE))

    def mask_tile(j, seen):
        kt = key_scr[j]
        eq = kt == tau_b
        both = _dot(jnp.where(eq, 1.0, 0.0).astype(BF16), pfx_ref[...])
        sel = (kt > tau_b) | (eq & (both[:, :LANE] + seen < take_b))
        m_ref[j] = jnp.where(sel, 0.0, NEG)
        return seen + both[:, LANE:]

    lax.fori_loop(0, nt, mask_tile, jnp.zeros((bb, LANE), F32))


def _sample_topk(scores, pfx, n_sel):
    nt, db, _ = scores.shape
    bb = min(db, 32)
    spec = pl.BlockSpec((nt, bb, LANE), lambda i: (0, i, 0))
    return pl.pallas_call(
        functools.partial(_sample_topk_kernel, n_sel=n_sel),
        out_shape=jax.ShapeDtypeStruct((nt, db, LANE), F32),
        grid=(db // bb,),
        in_specs=[spec, pl.BlockSpec(pfx.shape, lambda i: (0, 0))],
        out_specs=spec,
        scratch_shapes=[pltpu.VMEM((nt, bb, LANE), I32)],
        compiler_params=_params("arbitrary"),
        name="sample_topk",
    )(scores, pfx)


def _softmax_step(lg, values_t, state):
    m, l, acc = state
    m_new = jnp.maximum(m, jnp.max(lg, axis=1, keepdims=True))
    a = jnp.exp(m - m_new)
    p = jnp.exp(lg - m_new)
    return m_new, l * a + jnp.sum(p, axis=1, keepdims=True), acc * a + _nt_dot(p.astype(BF16), values_t)


def _softmax_init():
    return jnp.full((8, 1), NEG, F32), jnp.zeros((8, 1), F32), jnp.zeros((8, LANE), F32)


def _softmax_finish(lg_new, row_new, state):
    m, l, acc = state
    m_new = jnp.maximum(m, lg_new)
    a = jnp.exp(m - m_new)
    p = jnp.exp(lg_new - m_new)
    return (acc * a + p * row_new) / (l * a + p)


def _sample_a_kernel(pt_ref, rb_ref, q_ref, kvn_ref, msk_ref, cache_ref, o_ref, buf, sem, *, layer, n_pages, ch):
    q = q_ref[0]
    lane = lax.broadcasted_iota(I32, (1, LANE), 1)
    far, last, new = [], [], []
    for h in range(8):
        hh = min(h, A_HEADS - 1)
        far.append(jnp.full((1, LANE), rb_ref[N_BUCKETS - 1, hh], F32))
        last.append(_t5_bias_tile(rb_ref, hh, PAGE_SIZE - lane))
        new.append(jnp.full((1, 1), rb_ref[0, hh], F32))
    far, last, new = (jnp.concatenate(v, axis=0) for v in (far, last, new))

    def compute(c, kv_t, state):
        lg = _lane_tiles(_dot(q, kv_t[:HEAD_DIM]), ch)
        add = [msk_ref[c * ch + k, 0] + jnp.where(c * ch + k == n_pages - 1, last, far) for k in range(ch)]
        lg = jnp.concatenate([t + a for t, a in zip(lg, add)], axis=1)
        return _softmax_step(lg, kv_t, state)

    state = _paged_loop(pt_ref, cache_ref, layer, buf, sem, n_pages // ch, ch, compute, _softmax_init())
    row_new = kvn_ref[0].astype(F32)
    lg_new = (jnp.sum(q.astype(F32) * row_new[:, :HEAD_DIM], axis=1, keepdims=True) + new
              + msk_ref[n_pages, 0][:, 0:1])
    o_ref[0] = _softmax_finish(lg_new, row_new, state)


def _sample_a(page_table, cache, layer, rel_bias, q, kv_new, mask, ch):
    db, n_pages = page_table.shape
    per_b = lambda shape: pl.BlockSpec((1,) + shape, lambda b, pt: (b, 0, 0))
    mask4 = mask.reshape(n_pages + 1, db, 1, LANE)
    return _paged_call(
        functools.partial(_sample_a_kernel, layer=layer, n_pages=n_pages, ch=ch), "sample_a",
        page_table, cache, (rel_bias, q, kv_new, mask4),
        [pl.BlockSpec(memory_space=pltpu.SMEM), per_b((8, HEAD_DIM)), per_b((1, LANE)),
         pl.BlockSpec((n_pages + 1, 1, 1, LANE), lambda b, pt: (0, b, 0, 0))],
        jax.ShapeDtypeStruct((db, 8, LANE), F32), per_b((8, LANE)), ch)


def _split_bf16(x):
    hi = x.astype(BF16)
    return hi, (x - hi.astype(F32)).astype(BF16)


def _sample_b_kernel(pt_ref, q_ref, sfx_ref, cross_ref, cache_ref, o_ref, buf, sem, *, layer, n_pages, ch):
    b, nb = pl.program_id(0), pl.num_programs(0)
    n_chunks = n_pages // ch
    q = q_ref[0]

    def start(seq, c, slot):
        for cp in _page_copies(pt_ref, cache_ref, layer, buf, sem, slot, ch, lambda k: pt_ref[seq, c * ch + k]):
            cp.start()

    def wait(slot):
        for cp in _page_copies(pt_ref, cache_ref, layer, buf, sem, slot, ch, lambda k: 0):
            cp.wait()

    def compute(kv_t, right, acc):
        z = jnp.concatenate(_lane_tiles(_dot(q, kv_t[:2 * HEAD_DIM]), ch), axis=0)
        ls = _log_sigmoid(z)
        hi, lo = _split_bf16(ls - z)
        cum = _dot(hi, sfx_ref[...]) + _dot(lo, sfx_ref[...])
        tot = cum[:, LANE:]
        thi, tlo = _split_bf16(tot)
        off = _dot(cross_ref[...], thi) + _dot(cross_ref[...], tlo)
        w = jnp.exp(ls + cum[:, :LANE] + off + jnp.concatenate([right] * ch, axis=0))
        wl = jnp.concatenate([w[k * 8:(k + 1) * 8] for k in range(ch)], axis=1).astype(BF16)
        return right + off[:8] + tot[:8], acc + _nt_dot(wl, kv_t[2 * HEAD_DIM:])

    @pl.when(b == 0)
    def _():
        start(0, n_chunks - 1, 0)

    @pl.when(b + 1 < nb)
    def _():
        start(b + 1, n_chunks - 1, (b + 1) & 1)

    wait(b & 1)
    zero = jnp.zeros((8, LANE), F32)
    right, acc = compute(_chunk_slab(buf.at[b & 1], ch), zero, zero)

    def more(s):
        c, right, _ = s
        return (c >= 0) & (jnp.max(right) > SURVIVAL_CUTOFF)

    def older(s):
        c, right, acc = s
        start(b, c, 2)
        wait(2)
        right, acc = compute(_chunk_slab(buf.at[2], ch), right, acc)
        return c - 1, right, acc

    _, _, acc = lax.while_loop(more, older, (n_chunks - 2, right, acc))
    o_ref[0] = acc


def _sample_b(page_table, cache, layer, q, sfx, cross, ch):
    db, _ = page_table.shape
    per_b = lambda shape: pl.BlockSpec((1,) + shape, lambda b, pt: (b, 0, 0))
    full2 = lambda a: pl.BlockSpec(a.shape, lambda b, pt: (0, 0))
    return _paged_call(
        functools.partial(_sample_b_kernel, layer=layer, n_pages=page_table.shape[1], ch=ch), "sample_b",
        page_table, cache, (q, sfx, cross), [per_b((8, LANE)), full2(sfx), full2(cross)],
        jax.ShapeDtypeStruct((db, 8, LANE), F32), per_b((8, LANE)), ch, slots=3)


def _sample_c_kernel(pt_ref, q_ref, rown_ref, cache_ref, o_ref, buf, sem, *, layer, n_pages, ch):
    q = q_ref[0]
    width = C_KV_RANK + C_ROPE_DIM

    def compute(c, rows_t, state):
        lg = _dot(q[:, :C_KV_RANK], rows_t[:C_KV_RANK]) + _dot(q[:, C_KV_RANK:width], rows_t[C_KV_RANK:])
        return _softmax_step(lg, rows_t[:C_KV_RANK], state)

    state = _paged_loop(pt_ref, cache_ref, layer, buf, sem, n_pages // ch, ch, compute, _softmax_init())
    row_new = rown_ref[0].astype(F32)
    lg_new = jnp.sum(q.astype(F32) * row_new, axis=1, keepdims=True)
    o_ref[0] = _softmax_finish(lg_new, row_new[:, :C_KV_RANK], state)


def _sample_c(page_table, cache, layer, q, row_new, ch):
    db, _ = page_table.shape
    per_b = lambda shape: pl.BlockSpec((1,) + shape, lambda b, pt: (b, 0, 0))
    return _paged_call(
        functools.partial(_sample_c_kernel, layer=layer, n_pages=page_table.shape[1], ch=ch), "sample_c",
        page_table, cache, (q, row_new), [per_b((8, 2 * LANE)), per_b((1, 2 * LANE))],
        jax.ShapeDtypeStruct((db, 8, LANE), F32), per_b((8, LANE)), ch)


def _c_up_kernel(ol_ref, wuv_ref, o_ref):
    o_ref[...] = jnp.concatenate([_dot(ol_ref[:, h, :].astype(BF16), wuv_ref[h]) for h in range(C_HEADS)], axis=1)


def _c_up(o_lat, wuv):
    db = o_lat.shape[0]
    return pl.pallas_call(
        _c_up_kernel,
        out_shape=jax.ShapeDtypeStruct((db, C_HEADS * HEAD_DIM), F32),
        name="sample_c_up",
    )(o_lat, wuv)


def _cross_page_matrix(ch):
    idx = jnp.arange(ch * 8)
    return ((idx[:, None] % 8 == idx[None, :] % 8) & (idx[None, :] // 8 > idx[:, None] // 8)).astype(BF16)


def _layer_weights(l, w_in, c_q_norm, c_kv_norm, w_uq, w_uk, w_uv, w_out):
    wukt = jnp.transpose(w_uk[l], (1, 2, 0)).astype(BF16)
    wuv = jnp.transpose(w_uv[l], (1, 0, 2)).astype(BF16)
    proj = (_permute_w_in(w_in[l]), c_q_norm[l].reshape(1, -1), c_kv_norm[l].reshape(1, -1),
            _permute_w_uq(w_uq[l]), wukt, _rope_select())
    return proj, wuv, w_out[l].astype(BF16)


def _prompt_layer(hp, t_out, proj_w, wuv, w_out, lg, lb, tabs, rel_bias, pfx, sfx, n_sel, alpha):
    rows = hp.shape[1] // 4
    (aq, akv32, akv16, aiq, aik32, aik16, aiw, gates, bq, bkv32, bkvh, cqf, crow32, crow16) = _project(
        hp, t_out, rows, proj_w, tabs)
    oa = _attn_a(rel_bias, aiq, aiw, aq, aik16, akv16, pfx, n_sel)
    ob = _attn_b(bq, bkvh, sfx)
    oc = _attn_c(cqf, crow16, wuv)
    hp = _merge(hp, gates, oa, ob, oc, w_out, lg, lb, rows, alpha)
    return hp, (akv32, aik32, bkv32, crow32)


def _sample_layer(hs, l, proj_w, wuv, w_out, lg, lb, tabs, rel_bias, pfx, sfx, cross, n_sel, alpha,
                  page_table, cache_a_kv, cache_a_kidx, cache_b_kv, cache_c_kv, ch):
    db = hs.shape[0]
    (aq, akv32, akv16, aiq, aik32, aik16, aiw, gates, bq, bkv32, bkvh, cqf, crow32, crow16) = _project(
        hs[None], db, db, proj_w, tabs)
    by_seq = lambda a: jnp.transpose(a[0], (1, 0, 2))

    scores = _sample_idx(page_table, cache_a_kidx, l, by_seq(aiq), aiw[0, :, IDX_DIM:IDX_DIM + IDX_HEADS, None],
                         aik16[0, :, None, :IDX_DIM], ch)
    mask = _sample_topk(scores, pfx, n_sel)
    qa = jnp.pad(by_seq(aq), ((0, 0), (0, 8 - A_HEADS), (0, 0)))
    oa = _sample_a(page_table, cache_a_kv, l, rel_bias, qa, akv16[0, :, None, :], mask, ch)
    oa = oa[:, :A_HEADS, HEAD_DIM:].reshape(db, A_HEADS * HEAD_DIM)

    qb = by_seq(bq)
    zb = jnp.zeros_like(qb[:, :2])
    qb = jnp.concatenate([jnp.concatenate([qb[:, :2], zb], axis=2), jnp.concatenate([zb, qb[:, 2:]], axis=2),
                          jnp.zeros((db, 8 - B_HEADS, LANE), BF16)], axis=1)
    ob = _sample_b(page_table, cache_b_kv, l, qb, sfx, cross, cross.shape[0] // 8)
    ob = jnp.concatenate([ob[:, 0, :HEAD_DIM], ob[:, 1, :HEAD_DIM], ob[:, 2, HEAD_DIM:], ob[:, 3, HEAD_DIM:]], axis=1)

    o_lat = _sample_c(page_table, cache_c_kv, l, by_seq(cqf), crow16[0, :, None, :], ch)
    oc = _c_up(o_lat, wuv)
    hs = _merge(hs[None], gates, oa[None], ob[None], oc[None], w_out, lg, lb, db, alpha)[0]
    return hs, (akv32[0], aik32[0], bkv32[0], crow32[0])


def kernel(x_prompt, x_sample, cache_a_kv, cache_a_kidx, cache_b_kv, cache_c_kv, page_table, meta_tokens, rel_bias,
           emb_ln_g, emb_ln_b, w_in, c_q_norm, c_kv_norm, w_uq, w_uk, w_uv, w_out, ln_g, ln_b):
    bp, seq, d = x_prompt.shape
    db, dec_seq, _ = x_sample.shape
    assert dec_seq == 1
    depth = w_in.shape[0]
    n_phys = cache_a_kv.shape[0]
    n_pages = page_table.shape[1]
    past = n_pages * PAGE_SIZE
    t_p = N_META + seq
    t_pad = -(-t_p // TILE) * TILE
    n_sel_p = min(TOPK_MAX, seq // 4)
    n_sel_s = min(TOPK_MAX, (past + dec_seq) // 4)
    alpha = (2.0 * depth) ** 0.25
    ch = min(16, n_pages // 2)

    meta = jnp.broadcast_to(meta_tokens[None], (bp, N_META, d))
    xp = jnp.concatenate([meta, x_prompt, jnp.zeros((bp, t_pad - t_p, d), F32)], axis=1).reshape(bp * t_pad, d)
    ln_rows = 4 * TILE if (bp * t_pad) % (4 * TILE) == 0 else TILE
    hp = _layernorm(xp, emb_ln_g, emb_ln_b, ln_rows).reshape(bp, t_pad, d)
    hs = _layernorm(x_sample.reshape(db, d), emb_ln_g, emb_ln_b, db)

    tabs_p = _rope_tables(jnp.arange(t_pad, dtype=I32))
    tabs_s = _rope_tables(jnp.full((db,), past, I32))
    pfx, sfx = _prefix_matrices()
    cross = _cross_page_matrix(min(4, n_pages // 2))
    ca_idx = jnp.transpose(cache_a_kidx, (0, 1, 3, 2))
    ca_kv = jnp.transpose(cache_a_kv, (0, 1, 3, 4, 2)).reshape(n_phys, depth, 2 * HEAD_DIM, PAGE_SIZE)
    cb_kv = jnp.transpose(cache_b_kv, (0, 1, 3, 4, 5, 2)).reshape(n_phys, depth, 4 * HEAD_DIM, PAGE_SIZE)
    cc_kv = jnp.transpose(cache_c_kv, (0, 1, 3, 2))

    rows_p, rows_s = [], []
    for l in range(depth):
        proj_w, wuv, wo = _layer_weights(l, w_in, c_q_norm, c_kv_norm, w_uq, w_uk, w_uv, w_out)
        hp, rp = _prompt_layer(hp, t_p, proj_w, wuv, wo, ln_g[l], ln_b[l], tabs_p, rel_bias, pfx, sfx, n_sel_p, alpha)
        hs, rs = _sample_layer(hs, l, proj_w, wuv, wo, ln_g[l], ln_b[l], tabs_s, rel_bias, pfx, sfx, cross, n_sel_s,
                               alpha, page_table, ca_kv, ca_idx, cb_kv, cc_kv, ch)
        rows_p.append(rp)
        rows_s.append(rs)

    def stack(rows, k, lead, tail):
        return jnp.stack([r[k].reshape(lead + tail) for r in rows], axis=1)

    lp, ls = (bp, t_p), (db, dec_seq)
    return (hp[:, N_META:t_p], hs[:, None, :],
            stack(rows_p, 0, lp, (2, HEAD_DIM)), stack(rows_p, 1, lp, (IDX_DIM,)),
            stack(rows_p, 2, lp, (2, B_KV_HEADS, HEAD_DIM)), stack(rows_p, 3, lp, (C_KV_RANK + C_ROPE_DIM,)),
            stack(rows_s, 0, ls, (2, HEAD_DIM)), stack(rows_s, 1, ls, (IDX_DIM,)),
            stack(rows_s, 2, ls, (2, B_KV_HEADS, HEAD_DIM)), stack(rows_s, 3, ls, (C_KV_RANK + C_ROPE_DIM,)))
```

```python
import functools
import math

import jax
import jax.numpy as jnp
from jax import lax
from jax.experimental import pallas as pl
from jax.experimental.pallas import tpu as pltpu

F32, BF16, I32 = jnp.float32, jnp.bfloat16, jnp.int32

N_META = 16
HEAD_DIM = 64
A_HEADS = 4
IDX_HEADS = 8
IDX_DIM = 64
TOPK_MAX = 256
B_HEADS = 4
B_KV_HEADS = 2
C_HEADS = 8
C_NOPE_DIM = 64
C_ROPE_DIM = 32
C_KV_RANK = 128
N_BUCKETS = 32
MAX_DISTANCE = 128
ROPE_THETA = 10000.0
PAGE_SIZE = 128
LN_EPS = 1e-5
RMS_EPS = 1e-6

LANE = 128
TILE = 128
VMEM_LIMIT = 56 * 1024 * 1024
NEG = -1e30
SURVIVAL_CUTOFF = -100.0
PREFETCH_DEPTH = 3
PAGED_SLOTS = PREFETCH_DEPTH + 1
INT_MIN = -2 ** 31
C_SCALE = (C_NOPE_DIM + C_ROPE_DIM) ** -0.5
IW_SCALE = IDX_HEADS ** -0.5 * IDX_DIM ** -0.5

SEG_AQ = (0, 256)
SEG_AKV = (256, 384)
SEG_AIQ = (384, 896)
SEG_AIK = (896, 1024)
SEG_G = (1024, 2048)
SEG_BQ = (2048, 2304)
SEG_BKV = (2304, 2560)
SEG_CQ = (2560, 2816)
SEG_CKV = (2816, 2944)
SEG_CKR = (2944, 3072)
IN_PERM_WIDTH = 3072


def _params(*sem):
    return pltpu.CompilerParams(dimension_semantics=sem, vmem_limit_bytes=VMEM_LIMIT)


def _nt_dot(a, b):
    return lax.dot_general(a, b, (((1,), (1,)), ((), ())), preferred_element_type=F32)


def _dot(a, b):
    return jnp.dot(a, b, preferred_element_type=F32)


def _ln_kernel(x_ref, g_ref, b_ref, o_ref):
    x = x_ref[...]
    mu = jnp.mean(x, -1, keepdims=True)
    xc = x - mu
    var = jnp.mean(xc * xc, -1, keepdims=True)
    o_ref[...] = xc * lax.rsqrt(var + LN_EPS) * g_ref[...] + b_ref[...]


def _layernorm(x2d, g, b, rows):
    n, d = x2d.shape
    return pl.pallas_call(
        _ln_kernel,
        out_shape=jax.ShapeDtypeStruct((n, d), F32),
        grid=(n // rows,),
        in_specs=[pl.BlockSpec((rows, d), lambda i: (i, 0)),
                  pl.BlockSpec((1, d), lambda i: (0, 0)),
                  pl.BlockSpec((1, d), lambda i: (0, 0))],
        out_specs=pl.BlockSpec((rows, d), lambda i: (i, 0)),
        compiler_params=_params("arbitrary"),
        name="embed_ln",
    )(x2d, g.reshape(1, d), b.reshape(1, d))


def _project_kernel(x_ref, w_ref, qn_ref, kvn_ref, wuq_ref, wukt_ref, sel_ref,
                    cq_ref, sq_ref, ck_ref, sa_ref, sb_ref,
                    aq_ref, akv32_ref, akv16_ref, aiq_ref, aik32_ref, aik16_ref, aiw_ref, g_ref,
                    bq_ref, bkv32_ref, bkvh_ref, cqf_ref, crow32_ref, crow16_ref):
    xb = x_ref[0].astype(BF16)

    def seg(ab):
        return _dot(xb, w_ref[:, ab[0]:ab[1]])

    s = seg(SEG_AQ)
    for h in range(A_HEADS):
        aq_ref[0, h] = (s[:, h * 64:(h + 1) * 64] * HEAD_DIM ** -0.5).astype(BF16)
    s = seg(SEG_AKV)
    akv32_ref[0] = s
    akv16_ref[0] = s.astype(BF16)
    s = seg(SEG_AIQ)
    for h in range(IDX_HEADS):
        aiq_ref[0, h] = s[:, h * 64:(h + 1) * 64].astype(BF16)
    s = seg(SEG_AIK)
    aik32_ref[0] = s[:, :IDX_DIM]
    aik16_ref[0] = s.astype(BF16)
    aiw_ref[0] = s * IW_SCALE
    g_ref[0] = seg(SEG_G)
    s = seg(SEG_BQ)
    for h in range(B_HEADS):
        bq_ref[0, h] = (s[:, h * 64:(h + 1) * 64] * HEAD_DIM ** -0.5).astype(BF16)
    s = seg(SEG_BKV)
    bkv32_ref[0] = s
    for h in range(2 * B_KV_HEADS):
        bkvh_ref[0, h] = s[:, h * 64:(h + 1) * 64].astype(BF16)

    cq = seg(SEG_CQ)
    cqn = cq * lax.rsqrt(jnp.mean(cq * cq, -1, keepdims=True) + RMS_EPS) * qn_ref[...]
    cu = _dot(cqn.astype(BF16), wuq_ref[...])
    r1, r2 = cu[:, 512:640], cu[:, 640:768]
    c, sn = cq_ref[...], sq_ref[...]
    rot = (jnp.concatenate([r1 * c - r2 * sn, r1 * sn + r2 * c], axis=1) * C_SCALE).astype(BF16)
    for h in range(C_HEADS):
        qlat = _dot(cu[:, h * 64:(h + 1) * 64].astype(BF16), wukt_ref[h]) * C_SCALE
        qrope = _dot(rot, sel_ref[h])
        cqf_ref[0, h] = jnp.concatenate([qlat, qrope], axis=1).astype(BF16)

    ckv = seg(SEG_CKV)
    ckvn = ckv * lax.rsqrt(jnp.mean(ckv * ckv, -1, keepdims=True) + RMS_EPS) * kvn_ref[...]
    kr = seg(SEG_CKR)
    krot = kr * ck_ref[...] + pltpu.roll(kr, LANE - 16, 1) * sa_ref[...] + pltpu.roll(kr, 16, 1) * sb_ref[...]
    crow32_ref[0, :, 0:C_KV_RANK] = ckvn
    crow32_ref[0, :, C_KV_RANK:C_KV_RANK + C_ROPE_DIM] = krot[:, :C_ROPE_DIM]
    crow16_ref[0] = jnp.concatenate([ckvn, krot], axis=1).astype(BF16)


def _project(x, t_out, rows, wts, tabs):
    bn, t, d = x.shape
    nt = t // rows
    w_in, qn, kvn, wuq, wukt, sel = wts

    def full(a):
        nd = a.ndim
        return pl.BlockSpec(a.shape, lambda b, i, _n=nd: (0,) * _n)

    def rows3(w):
        return pl.BlockSpec((1, rows, w), lambda b, i: (b, i, 0))

    def heads4(nh, w):
        return pl.BlockSpec((1, nh, rows, w), lambda b, i: (b, 0, i, 0))

    tab_spec = pl.BlockSpec((rows, LANE), lambda b, i: (i, 0))
    out_shape = [
        jax.ShapeDtypeStruct((bn, A_HEADS, t, 64), BF16),
        jax.ShapeDtypeStruct((bn, t_out, 128), F32),
        jax.ShapeDtypeStruct((bn, t, 128), BF16),
        jax.ShapeDtypeStruct((bn, IDX_HEADS, t, 64), BF16),
        jax.ShapeDtypeStruct((bn, t_out, 64), F32),
        jax.ShapeDtypeStruct((bn, t, 128), BF16),
        jax.ShapeDtypeStruct((bn, t, 128), F32),
        jax.ShapeDtypeStruct((bn, t, 1024), F32),
        jax.ShapeDtypeStruct((bn, B_HEADS, t, 64), BF16),
        jax.ShapeDtypeStruct((bn, t_out, 256), F32),
        jax.ShapeDtypeStruct((bn, 4, t, 64), BF16),
        jax.ShapeDtypeStruct((bn, C_HEADS, t, 256), BF16),
        jax.ShapeDtypeStruct((bn, t_out, 160), F32),
        jax.ShapeDtypeStruct((bn, t, 256), BF16),
    ]
    out_specs = [heads4(A_HEADS, 64), rows3(128), rows3(128), heads4(IDX_HEADS, 64), rows3(64), rows3(128),
                 rows3(128), rows3(1024), heads4(B_HEADS, 64), rows3(256), heads4(4, 64),
                 heads4(C_HEADS, 256), rows3(160), rows3(256)]
    return pl.pallas_call(
        _project_kernel,
        out_shape=out_shape,
        grid=(bn, nt),
        in_specs=[rows3(d), full(w_in), full(qn), full(kvn), full(wuq), full(wukt), full(sel)] + [tab_spec] * 5,
        out_specs=out_specs,
        compiler_params=_params("arbitrary", "arbitrary"),
        name="project",
    )(x, w_in, qn, kvn, wuq, wukt, sel, *tabs)


def _merge_kernel(x_ref, g_ref, oa_ref, ob_ref, oc_ref, w_ref, lg_ref, lb_ref, y_ref, *, alpha):
    g = g_ref[0]
    o = jnp.concatenate([oa_ref[0], ob_ref[0], oc_ref[0]], axis=1)
    mixed = (o * (g * jax.nn.sigmoid(g))).astype(BF16)
    z = alpha * x_ref[0] + _dot(mixed, w_ref[...])
    mu = jnp.mean(z, -1, keepdims=True)
    zc = z - mu
    var = jnp.mean(zc * zc, -1, keepdims=True)
    y_ref[0] = zc * lax.rsqrt(var + LN_EPS) * lg_ref[...] + lb_ref[...]


def _merge(x, gates, oa, ob, oc, w_out, lg, lb, rows, alpha):
    bn, t, d = x.shape

    def rows3(w):
        return pl.BlockSpec((1, rows, w), lambda b, i: (b, i, 0))

    def full(a):
        nd = a.ndim
        return pl.BlockSpec(a.shape, lambda b, i, _n=nd: (0,) * _n)

    lg2, lb2 = lg.reshape(1, d), lb.reshape(1, d)
    return pl.pallas_call(
        functools.partial(_merge_kernel, alpha=alpha),
        out_shape=jax.ShapeDtypeStruct((bn, t, d), F32),
        grid=(bn, t // rows),
        in_specs=[rows3(d), rows3(1024), rows3(256), rows3(256), rows3(512), full(w_out), full(lg2), full(lb2)],
        out_specs=rows3(d),
        compiler_params=_params("arbitrary", "arbitrary"),
        name="merge",
    )(x, gates, oa, ob, oc, w_out, lg2, lb2)


def _t5_bias_tile(rb_ref, h, dist):
    max_exact = N_BUCKETS // 2
    lg = jnp.log(jnp.maximum(dist, 1).astype(F32) / max_exact) / math.log(MAX_DISTANCE / max_exact)
    large = jnp.minimum(max_exact + (lg * (N_BUCKETS - max_exact)).astype(I32), N_BUCKETS - 1)
    bucket = jnp.where(dist < max_exact, dist, large)
    out = jnp.zeros(dist.shape, F32)
    for k in range(N_BUCKETS):
        out = jnp.where(bucket == k, rb_ref[k, h], out)
    return out


def _sortable(score):
    bits = pltpu.bitcast(score + 0.0, I32)
    return bits ^ ((bits >> 31) & 0x7FFFFFFF)


def _kth_largest(count_ge, shape, k):
    def body(t, tau):
        bit = lax.shift_left(jnp.int32(1), 31 - t)
        cand = tau + bit
        return jnp.where(count_ge(cand) >= k, cand, tau)

    return lax.fori_loop(0, 32, body, jnp.full(shape, INT_MIN, I32))


def _attn_a_kernel(rb_ref, iq_ref, iw_ref, q_ref, ik_ref, kv_ref, pfx_ref, o_ref,
                   bias_scr, wb_scr, key_scr, keyt_scr, msk_scr, lg_scr, mx_scr, acc_scr, ls_scr, *, n_sel):
    b, i = pl.program_id(0), pl.program_id(1)
    nkt = i + 1
    row = lax.broadcasted_iota(I32, (TILE, TILE), 0)
    col = lax.broadcasted_iota(I32, (TILE, TILE), 1)

    @pl.when((b == 0) & (i == 0))
    def _():
        for h in range(A_HEADS):
            bias_scr[h, 0] = _t5_bias_tile(rb_ref, h, jnp.maximum(row - col, 0))
            bias_scr[h, 1] = _t5_bias_tile(rb_ref, h, TILE + row - col)
            bias_scr[h, 2] = _t5_bias_tile(rb_ref, h, jnp.full((TILE, TILE), 2 * TILE, I32))

    iw = iw_ref[0]
    for h in range(IDX_HEADS):
        wb_scr[h] = jnp.broadcast_to(iw[:, IDX_DIM + h:IDX_DIM + h + 1], (TILE, TILE))
    iqs = iq_ref[0].reshape(IDX_HEADS * TILE, IDX_DIM)

    def score_keys(start, width):
        kt = ik_ref[0, pl.ds(pl.multiple_of(start, TILE), width), :][:, :IDX_DIM]
        s = _nt_dot(iqs, kt)
        acc = jnp.zeros((TILE, width), F32)
        for h in range(IDX_HEADS):
            wb = wb_scr[h] if width == TILE else jnp.concatenate([wb_scr[h]] * (width // TILE), axis=1)
            acc = acc + wb * jnp.maximum(s[h * TILE:(h + 1) * TILE], 0.0)
        return _sortable(acc)

    def score_pair(jj, carry):
        keys = score_keys(jj * 2 * TILE, 2 * TILE)
        key_scr[2 * jj] = keys[:, :TILE]
        key_scr[2 * jj + 1] = keys[:, TILE:]
        return carry

    def score_tile(j, carry):
        key_scr[j] = score_keys(j * TILE, TILE)
        return carry

    lax.fori_loop(0, nkt // 2, score_pair, 0)
    lax.fori_loop(2 * (nkt // 2), nkt, score_tile, 0)
    key_scr[i] = jnp.where(col <= row, key_scr[i], INT_MIN)

    def transpose_tile(j, carry):
        keyt_scr[j] = key_scr[j].T
        return carry

    lax.fori_loop(0, nkt, transpose_tile, 0)

    def count(pred):
        def tile_count(j):
            return jnp.sum(jnp.where(pred(keyt_scr[j]), 1, 0).reshape(TILE // 8, 8, TILE), axis=0)

        c = lax.fori_loop(0, nkt // 2, lambda jj, c: c + tile_count(2 * jj) + tile_count(2 * jj + 1),
                          jnp.zeros((8, TILE), I32))
        c = c + jnp.where(nkt % 2 == 1, tile_count(nkt - 1), 0)
        return jnp.sum(c, axis=0, keepdims=True)

    tau_t = _kth_largest(lambda cand: count(lambda kt: kt >= cand), (1, TILE), n_sel)
    tau_t = jnp.maximum(tau_t, INT_MIN + 1)
    take_t = (n_sel - count(lambda kt: kt > tau_t)).astype(F32)
    tau_b = jnp.broadcast_to(tau_t, (TILE, TILE)).T
    take_b = jnp.broadcast_to(take_t, (TILE, TILE)).T

    n_eq_t = count(lambda kt: kt == tau_t).astype(F32)
    split_ties = jnp.max(jnp.where(n_eq_t > take_t, 1, 0)) > 0

    def mask_tile_ties(j, seen):
        kt = key_scr[j]
        eq = kt == tau_b
        both = _dot(jnp.where(eq, 1.0, 0.0).astype(BF16), pfx_ref[...])
        sel = (kt > tau_b) | (eq & (both[:, :TILE] + seen < take_b))
        msk_scr[j] = jnp.where(sel, 0.0, NEG)
        return seen + both[:, TILE:]

    def mask_tile(j, carry):
        msk_scr[j] = jnp.where(key_scr[j] >= tau_b, 0.0, NEG)
        return carry

    @pl.when(split_ties)
    def _():
        lax.fori_loop(0, nkt, mask_tile_ties, jnp.zeros((TILE, TILE), F32))

    @pl.when(jnp.logical_not(split_ties))
    def _():
        lax.fori_loop(0, nkt, mask_tile, 0)

    qs = q_ref[0].reshape(A_HEADS * TILE, HEAD_DIM)
    mx_scr[...] = jnp.full(mx_scr.shape, NEG, F32)

    def logits_keys(j0, ntile):
        kvt = kv_ref[0, pl.ds(pl.multiple_of(j0 * TILE, TILE), ntile * TILE), :]
        lg = _nt_dot(qs, kvt[:, :HEAD_DIM])
        for u in range(ntile):
            j = j0 + u
            bidx = jnp.minimum(i - j, 2)
            madd = msk_scr[j]
            for h in range(A_HEADS):
                sl = slice(h * TILE, (h + 1) * TILE)
                lgh = lg[sl, u * TILE:(u + 1) * TILE] + bias_scr[h, bidx] + madd
                lg_scr[j, sl] = lgh
                mx_scr[sl] = jnp.maximum(mx_scr[sl], lgh)

    def pv_keys(j0, ntile):
        kvt = kv_ref[0, pl.ds(pl.multiple_of(j0 * TILE, TILE), ntile * TILE), :]
        ps = [jnp.exp(lg_scr[j0 + u] - m_b) for u in range(ntile)]
        ls_scr[...] += sum(ps[1:], ps[0])
        acc_scr[...] += _dot(jnp.concatenate(ps, axis=1).astype(BF16), kvt)

    def in_pairs(fn):
        def pair(jj, carry):
            fn(2 * jj, 2)
            return carry

        def single(j, carry):
            fn(j, 1)
            return carry

        lax.fori_loop(0, nkt // 2, pair, 0)
        lax.fori_loop(2 * (nkt // 2), nkt, single, 0)

    in_pairs(logits_keys)
    m_b = jnp.broadcast_to(jnp.max(mx_scr[...], axis=1, keepdims=True), mx_scr.shape)
    acc_scr[...] = jnp.zeros(acc_scr.shape, F32)
    ls_scr[...] = jnp.zeros(ls_scr.shape, F32)
    in_pairs(pv_keys)
    o = acc_scr[...] / jnp.sum(ls_scr[...], axis=1, keepdims=True)
    o_ref[0] = jnp.concatenate([o[h * TILE:(h + 1) * TILE, HEAD_DIM:] for h in range(A_HEADS)], axis=1)


def _attn_a(rel_bias, aiq, aiw, aq, aik16, akv16, pfx, n_sel):
    bn, _, t, _ = aq.shape
    nt = t // TILE
    return pl.pallas_call(
        functools.partial(_attn_a_kernel, n_sel=n_sel),
        out_shape=jax.ShapeDtypeStruct((bn, t, A_HEADS * HEAD_DIM), F32),
        grid=(bn, nt),
        in_specs=[pl.BlockSpec(memory_space=pltpu.SMEM),
                  pl.BlockSpec((1, IDX_HEADS, TILE, 64), lambda b, i: (b, 0, i, 0)),
                  pl.BlockSpec((1, TILE, 128), lambda b, i: (b, i, 0)),
                  pl.BlockSpec((1, A_HEADS, TILE, 64), lambda b, i: (b, 0, i, 0)),
                  pl.BlockSpec((1, t, 128), lambda b, i: (b, 0, 0)),
                  pl.BlockSpec((1, t, 128), lambda b, i: (b, 0, 0)),
                  pl.BlockSpec((TILE, 2 * TILE), lambda b, i: (0, 0))],
        out_specs=pl.BlockSpec((1, TILE, A_HEADS * HEAD_DIM), lambda b, i: (b, i, 0)),
        scratch_shapes=[pltpu.VMEM((A_HEADS, 3, TILE, TILE), F32),
                        pltpu.VMEM((IDX_HEADS, TILE, TILE), F32),
                        pltpu.VMEM((nt, TILE, TILE), I32),
                        pltpu.VMEM((nt, TILE, TILE), I32),
                        pltpu.VMEM((nt, TILE, TILE), F32),
                        pltpu.VMEM((nt, A_HEADS * TILE, TILE), F32),
                        pltpu.VMEM((A_HEADS * TILE, TILE), F32),
                        pltpu.VMEM((A_HEADS * TILE, TILE), F32),
                        pltpu.VMEM((A_HEADS * TILE, TILE), F32)],
        compiler_params=_params("arbitrary", "arbitrary"),
        name="attn_a_prompt",
    )(rel_bias, aiq, aiw, aq, aik16, akv16, pfx)


def _log_sigmoid(z):
    return jnp.minimum(z, 0.0) - jnp.log1p(jnp.exp(-jnp.abs(z)))


def _attn_b_kernel(q_ref, kv_ref, sfx_ref, o_ref, right_scr, acc_scr):
    i = pl.program_id(1)
    rows = (B_HEADS // B_KV_HEADS) * TILE
    row = lax.broadcasted_iota(I32, (rows, TILE), 0) & (TILE - 1)
    col = lax.broadcasted_iota(I32, (rows, TILE), 1)
    qs = [q_ref[0, 2 * g:2 * g + 2].reshape(rows, HEAD_DIM) for g in range(B_KV_HEADS)]

    def tile(j, diagonal):
        ks = pl.ds(pl.multiple_of(j * TILE, TILE), TILE)
        groups = range(B_KV_HEADS)
        rights = [right_scr[g] for g in groups]
        zs = [_nt_dot(qs[g], kv_ref[0, g, ks, :]) for g in groups]
        lss = [_log_sigmoid(z) for z in zs]
        lnegs = [ls - z for ls, z in zip(lss, zs)]
        if diagonal:
            lnegs = [jnp.where(col < row, x, 0.0) for x in lnegs]
        splits = [_split_bf16(x) for x in lnegs]
        cums = [_dot(hi, sfx_ref[...]) + _dot(lo, sfx_ref[...]) for hi, lo in splits]
        ws = [jnp.exp(ls + cum[:, :TILE] + r) for ls, cum, r in zip(lss, cums, rights)]
        if diagonal:
            ws = [jnp.where(col < row, w, 0.0) for w in ws]
        pvs = [_dot(ws[g].astype(BF16), kv_ref[0, B_KV_HEADS + g, ks, :]) for g in groups]
        for g in groups:
            acc_scr[g] += pvs[g]
            right_scr[g] = rights[g] + cums[g][:, TILE:]

    right_scr[...] = jnp.zeros(right_scr.shape, F32)
    acc_scr[...] = jnp.zeros(acc_scr.shape, F32)
    tile(i, True)

    def more(j):
        return (j >= 0) & (jnp.max(right_scr[...]) > SURVIVAL_CUTOFF)

    def older(j):
        tile(j, False)
        return j - 1

    lax.while_loop(more, older, i - 1)
    o_ref[0] = jnp.concatenate([acc_scr[g, r * TILE:(r + 1) * TILE] for g in range(B_KV_HEADS) for r in range(2)],
                               axis=1)


def _attn_b(bq, bkvh, sfx):
    bn, _, t, _ = bq.shape
    return pl.pallas_call(
        _attn_b_kernel,
        out_shape=jax.ShapeDtypeStruct((bn, t, B_HEADS * HEAD_DIM), F32),
        grid=(bn, t // TILE),
        in_specs=[pl.BlockSpec((1, B_HEADS, TILE, 64), lambda b, i: (b, 0, i, 0)),
                  pl.BlockSpec((1, 4, t, 64), lambda b, i: (b, 0, 0, 0)),
                  pl.BlockSpec((TILE, 2 * TILE), lambda b, i: (0, 0))],
        out_specs=pl.BlockSpec((1, TILE, B_HEADS * HEAD_DIM), lambda b, i: (b, i, 0)),
        scratch_shapes=[pltpu.VMEM((B_KV_HEADS, 2 * TILE, TILE), F32),
                        pltpu.VMEM((B_KV_HEADS, 2 * TILE, HEAD_DIM), F32)],
        compiler_params=_params("arbitrary", "arbitrary"),
        name="attn_b_prompt",
    )(bq, bkvh, sfx)


def _attn_c_kernel(q_ref, c_ref, wuv_ref, o_ref, lg_scr, mx_scr, acc_scr, ls_scr):
    i = pl.program_id(1)
    rows = C_HEADS * TILE
    row = lax.broadcasted_iota(I32, (rows, TILE), 0) & (TILE - 1)
    col = lax.broadcasted_iota(I32, (rows, TILE), 1)
    qs = q_ref[0].reshape(rows, 2 * LANE)
    mx_scr[...] = jnp.full(mx_scr.shape, NEG, F32)

    n_pairs = i // 2

    def logits_pair(jj, carry):
        ct = c_ref[0, pl.ds(pl.multiple_of(jj * 2 * TILE, 2 * TILE), 2 * TILE), :]
        lg = _nt_dot(qs, ct)
        lg_scr[2 * jj] = lg[:, :TILE]
        lg_scr[2 * jj + 1] = lg[:, TILE:]
        mx_scr[...] = jnp.maximum(mx_scr[...], jnp.maximum(lg[:, :TILE], lg[:, TILE:]))
        return carry

    def logits_tile(j, carry):
        ct = c_ref[0, pl.ds(pl.multiple_of(j * TILE, TILE), TILE), :]
        lg = jnp.where((col <= row) | (j < i), _nt_dot(qs, ct), NEG)
        lg_scr[j] = lg
        mx_scr[...] = jnp.maximum(mx_scr[...], lg)
        return carry

    lax.fori_loop(0, n_pairs, logits_pair, 0)
    lax.fori_loop(2 * n_pairs, i + 1, logits_tile, 0)
    m_b = jnp.broadcast_to(jnp.max(mx_scr[...], axis=1, keepdims=True), (rows, TILE))
    acc_scr[...] = jnp.zeros(acc_scr.shape, F32)
    ls_scr[...] = jnp.zeros(ls_scr.shape, F32)

    def pv_pair(jj, carry):
        ct = c_ref[0, pl.ds(pl.multiple_of(jj * 2 * TILE, 2 * TILE), 2 * TILE), :]
        p0 = jnp.exp(lg_scr[2 * jj] - m_b)
        p1 = jnp.exp(lg_scr[2 * jj + 1] - m_b)
        ls_scr[...] += p0 + p1
        acc_scr[...] += _dot(jnp.concatenate([p0, p1], axis=1).astype(BF16), ct[:, :C_KV_RANK])
        return carry

    def pv_tile(j, carry):
        ct = c_ref[0, pl.ds(pl.multiple_of(j * TILE, TILE), TILE), :]
        p = jnp.exp(lg_scr[j] - m_b)
        ls_scr[...] += p
        acc_scr[...] += _dot(p.astype(BF16), ct[:, :C_KV_RANK])
        return carry

    lax.fori_loop(0, n_pairs, pv_pair, 0)
    lax.fori_loop(2 * n_pairs, i + 1, pv_tile, 0)
    ol = (acc_scr[...] / jnp.sum(ls_scr[...], axis=1, keepdims=True)).astype(BF16)
    o_ref[0] = jnp.concatenate([_dot(ol[h * TILE:(h + 1) * TILE], wuv_ref[h]) for h in range(C_HEADS)], axis=1)


def _attn_c(cqf, crow16, wuv):
    bn, _, t, _ = cqf.shape
    nt = t // TILE
    rows = C_HEADS * TILE
    return pl.pallas_call(
        _attn_c_kernel,
        out_shape=jax.ShapeDtypeStruct((bn, t, C_HEADS * HEAD_DIM), F32),
        grid=(bn, nt),
        in_specs=[pl.BlockSpec((1, C_HEADS, TILE, 256), lambda b, i: (b, 0, i, 0)),
                  pl.BlockSpec((1, t, 256), lambda b, i: (b, 0, 0)),
                  pl.BlockSpec(wuv.shape, lambda b, i: (0, 0, 0))],
        out_specs=pl.BlockSpec((1, TILE, C_HEADS * HEAD_DIM), lambda b, i: (b, i, 0)),
        scratch_shapes=[pltpu.VMEM((nt, rows, TILE), F32),
                        pltpu.VMEM((rows, TILE), F32),
                        pltpu.VMEM((rows, TILE), F32),
                        pltpu.VMEM((rows, TILE), F32)],
        compiler_params=_params("arbitrary", "arbitrary"),
        name="attn_c_prompt",
    )(cqf, crow16, wuv)


def _permute_w_in(w):
    d = w.shape[0]
    names = dict(a_q=(0, 256), a_k=(256, 320), a_v=(320, 384), a_iq=(384, 896), a_ik=(896, 960), a_iw=(960, 968),
                 a_g=(968, 1224), b_q=(1224, 1480), b_k=(1480, 1608), b_v=(1608, 1736), b_g=(1736, 1992),
                 c_q=(1992, 2248), c_kv=(2248, 2376), c_kr=(2376, 2408), c_g=(2408, 2920))

    def c(n):
        return w[:, names[n][0]:names[n][1]]

    z = lambda n: jnp.zeros((d, n), w.dtype)
    out = jnp.concatenate([c("a_q"), c("a_k"), c("a_v"), c("a_iq"), c("a_ik"), c("a_iw"), z(56),
                           c("a_g"), c("b_g"), c("c_g"), c("b_q"), c("b_k"), c("b_v"),
                           c("c_q"), c("c_kv"), c("c_kr"), z(96)], axis=1)
    assert out.shape[1] == IN_PERM_WIDTH
    return out.astype(BF16)


def _permute_w_uq(w):
    r = w.reshape(w.shape[0], C_HEADS, C_NOPE_DIM + C_ROPE_DIM)
    half = C_ROPE_DIM // 2
    nope = r[:, :, :C_NOPE_DIM].reshape(w.shape[0], -1)
    lo = r[:, :, C_NOPE_DIM:C_NOPE_DIM + half].reshape(w.shape[0], -1)
    hi = r[:, :, C_NOPE_DIM + half:].reshape(w.shape[0], -1)
    return jnp.concatenate([nope, lo, hi], axis=1).astype(BF16)


def _rope_select():
    half = C_ROPE_DIM // 2
    src = jnp.arange(2 * LANE)[:, None]
    dst = jnp.arange(LANE)[None, :]
    mats = []
    for h in range(C_HEADS):
        lo = (src == h * half + dst) & (dst < half)
        hi = (src == LANE + h * half + dst - half) & (dst >= half) & (dst < 2 * half)
        mats.append(lo | hi)
    return jnp.stack(mats).astype(BF16)


def _rope_tables(pos):
    half = C_ROPE_DIM // 2
    freq = ROPE_THETA ** (-jnp.arange(half, dtype=F32) / half)
    ang = pos.astype(F32)[:, None] * freq
    c, s = jnp.cos(ang), jnp.sin(ang)
    cq, sq = jnp.tile(c, (1, LANE // half)), jnp.tile(s, (1, LANE // half))
    z = jnp.zeros((pos.shape[0], LANE - 2 * half), F32)
    zh = jnp.zeros_like(c)
    ck = jnp.concatenate([c, c, z], axis=1)
    sa = jnp.concatenate([-s, zh, z], axis=1)
    sb = jnp.concatenate([zh, s, z], axis=1)
    return cq, sq, ck, sa, sb


def _prefix_matrices():
    a = jnp.arange(TILE)
    ones = jnp.ones((TILE, TILE), F32)
    pfx = jnp.concatenate([(a[:, None] < a[None, :]).astype(F32), ones], axis=1).astype(BF16)
    sfx = jnp.concatenate([(a[:, None] > a[None, :]).astype(F32), ones], axis=1).astype(BF16)
    return pfx, sfx


def _page_copies(pt_ref, cache_ref, layer, buf, sem, slot, ch, page_of):
    return [pltpu.make_async_copy(cache_ref.at[page_of(k), layer], buf.at[slot, k], sem.at[slot]) for k in range(ch)]


def _paged_loop(pt_ref, cache_ref, layer, buf, sem, n_chunks, ch, compute, init):
    b, nb = pl.program_id(0), pl.num_programs(0)

    def start(g):
        seq, c = g // n_chunks, g % n_chunks
        for cp in _page_copies(pt_ref, cache_ref, layer, buf, sem, g % PAGED_SLOTS, ch,
                               lambda k: pt_ref[seq, c * ch + k]):
            cp.start()

    def wait(slot):
        for cp in _page_copies(pt_ref, cache_ref, layer, buf, sem, slot, ch, lambda k: 0):
            cp.wait()

    @pl.when(b == 0)
    def _():
        for g in range(PREFETCH_DEPTH):
            start(g)

    def body(c, carry):
        g = b * n_chunks + c

        @pl.when(g + PREFETCH_DEPTH < nb * n_chunks)
        def _():
            start(g + PREFETCH_DEPTH)

        slot = g % PAGED_SLOTS
        wait(slot)
        return compute(c, _chunk_slab(buf.at[slot], ch), carry)

    return lax.fori_loop(0, n_chunks, body, init)


def _chunk_slab(view, ch):
    return jnp.concatenate([view[k] for k in range(ch)], axis=1).astype(BF16)


def _lane_tiles(x, n):
    return [x[:, k * LANE:(k + 1) * LANE] for k in range(n)]


def _paged_call(kernel, name, page_table, layer_cache, operands, operand_specs, out_shape, out_spec, ch,
                slots=PAGED_SLOTS):
    db = page_table.shape[0]
    feat = layer_cache.shape[2]
    grid_spec = pltpu.PrefetchScalarGridSpec(
        num_scalar_prefetch=1,
        grid=(db,),
        in_specs=list(operand_specs) + [pl.BlockSpec(memory_space=pl.ANY)],
        out_specs=out_spec,
        scratch_shapes=[pltpu.VMEM((slots, ch, feat, PAGE_SIZE), F32), pltpu.SemaphoreType.DMA((slots,))],
    )
    return pl.pallas_call(kernel, out_shape=out_shape, grid_spec=grid_spec,
                          compiler_params=_params("arbitrary"), name=name)(page_table, *operands, layer_cache)


def _sample_idx_kernel(pt_ref, iq_ref, w_ref, ikn_ref, cache_ref, out_ref, buf, sem, *, layer, n_pages, ch):
    iq = iq_ref[0]
    w = w_ref[0]

    def compute(c, keys_t, carry):
        sc = jnp.sum(jnp.maximum(_dot(iq, keys_t), 0.0) * w, axis=0, keepdims=True)
        for k, tile in enumerate(_lane_tiles(sc, ch)):
            out_ref[c * ch + k, 0] = tile
        return carry

    _paged_loop(pt_ref, cache_ref, layer, buf, sem, n_pages // ch, ch, compute, 0)
    s_new = jnp.sum(iq.astype(F32) * ikn_ref[0].astype(F32), axis=1, keepdims=True)
    s_new = jnp.sum(jnp.maximum(s_new, 0.0) * w, axis=0, keepdims=True)
    lane = lax.broadcasted_iota(I32, (1, LANE), 1)
    out_ref[n_pages, 0] = jnp.where(lane == 0, s_new, -jnp.inf)


def _sample_idx(page_table, cache, layer, iq, w, ik_new, ch):
    db, n_pages = page_table.shape
    per_b = lambda shape: pl.BlockSpec((1,) + shape, lambda b, pt: (b, 0, 0))
    out = _paged_call(
        functools.partial(_sample_idx_kernel, layer=layer, n_pages=n_pages, ch=ch), "sample_idx",
        page_table, cache, (iq, w, ik_new), [per_b((IDX_HEADS, IDX_DIM)), per_b((IDX_HEADS, 1)), per_b((1, IDX_DIM))],
        jax.ShapeDtypeStruct((n_pages + 1, db, 1, LANE), F32),
        pl.BlockSpec((n_pages + 1, 1, 1, LANE), lambda b, pt: (0, b, 0, 0)), ch)
    return out.reshape(n_pages + 1, db, LANE)


def _sample_topk_kernel(s_ref, pfx_ref, m_ref, key_scr, *, n_sel):
    nt, bb, _ = s_ref.shape

    def to_key(j, carry):
        key_scr[j] = _sortable(s_ref[j])
        return carry

    lax.fori_loop(0, nt, to_key, 0)

    def count(pred):
        c = lax.fori_loop(0, nt, lambda j, c: c + jnp.where(pred(key_scr[j]), 1, 0), jnp.zeros((bb, LANE), I32))
        return jnp.sum(c, axis=1, keepdims=True)

    def count_ge(cand):
        cb = jnp.broadcast_to(cand, (bb, LANE))
        return count(lambda kt: kt >= cb)

    tau = _kth_largest(count_ge, (bb, 1), n_sel)
    tau_b = jnp.broadcast_to(jnp.maximum(tau, INT_MIN + 1), (bb, LANE))
    take_b = jnp.broadcast_to((n_sel - count(lambda kt: kt > tau_b)).astype(F32), (bb, LANE))

    def mask_tile(j, seen):
        kt = key_scr[j]
        eq = kt == tau_b
        both = _dot(jnp.where(eq, 1.0, 0.0).astype(BF16), pfx_ref[...])
        sel = (kt > tau_b) | (eq & (both[:, :LANE] + seen < take_b))
        m_ref[j] = jnp.where(sel, 0.0, NEG)
        return seen + both[:, LANE:]

    lax.fori_loop(0, nt, mask_tile, jnp.zeros((bb, LANE), F32))


def _sample_topk(scores, pfx, n_sel):
    nt, db, _ = scores.shape
    bb = min(db, 32)
    spec = pl.BlockSpec((nt, bb, LANE), lambda i: (0, i, 0))
    return pl.pallas_call(
        functools.partial(_sample_topk_kernel, n_sel=n_sel),
        out_shape=jax.ShapeDtypeStruct((nt, db, LANE), F32),
        grid=(db // bb,),
        in_specs=[spec, pl.BlockSpec(pfx.shape, lambda i: (0, 0))],
        out_specs=spec,
        scratch_shapes=[pltpu.VMEM((nt, bb, LANE), I32)],
        compiler_params=_params("arbitrary"),
        name="sample_topk",
    )(scores, pfx)


def _softmax_step(lg, values_t, state):
    m, l, acc = state
    m_new = jnp.maximum(m, jnp.max(lg, axis=1, keepdims=True))
    a = jnp.exp(m - m_new)
    p = jnp.exp(lg - m_new)
    return m_new, l * a + jnp.sum(p, axis=1, keepdims=True), acc * a + _nt_dot(p.astype(BF16), values_t)


def _softmax_init():
    return jnp.full((8, 1), NEG, F32), jnp.zeros((8, 1), F32), jnp.zeros((8, LANE), F32)


def _softmax_finish(lg_new, row_new, state):
    m, l, acc = state
    m_new = jnp.maximum(m, lg_new)
    a = jnp.exp(m - m_new)
    p = jnp.exp(lg_new - m_new)
    return (acc * a + p * row_new) / (l * a + p)


def _sample_a_kernel(pt_ref, rb_ref, q_ref, kvn_ref, msk_ref, cache_ref, o_ref, buf, sem, *, layer, n_pages, ch):
    q = q_ref[0]
    lane = lax.broadcasted_iota(I32, (1, LANE), 1)
    far, last, new = [], [], []
    for h in range(8):
        hh = min(h, A_HEADS - 1)
        far.append(jnp.full((1, LANE), rb_ref[N_BUCKETS - 1, hh], F32))
        last.append(_t5_bias_tile(rb_ref, hh, PAGE_SIZE - lane))
        new.append(jnp.full((1, 1), rb_ref[0, hh], F32))
    far, last, new = (jnp.concatenate(v, axis=0) for v in (far, last, new))

    def compute(c, kv_t, state):
        lg = _lane_tiles(_dot(q, kv_t[:HEAD_DIM]), ch)
        add = [msk_ref[c * ch + k, 0] + jnp.where(c * ch + k == n_pages - 1, last, far) for k in range(ch)]
        lg = jnp.concatenate([t + a for t, a in zip(lg, add)], axis=1)
        return _softmax_step(lg, kv_t, state)

    state = _paged_loop(pt_ref, cache_ref, layer, buf, sem, n_pages // ch, ch, compute, _softmax_init())
    row_new = kvn_ref[0].astype(F32)
    lg_new = (jnp.sum(q.astype(F32) * row_new[:, :HEAD_DIM], axis=1, keepdims=True) + new
              + msk_ref[n_pages, 0][:, 0:1])
    o_ref[0] = _softmax_finish(lg_new, row_new, state)


def _sample_a(page_table, cache, layer, rel_bias, q, kv_new, mask, ch):
    db, n_pages = page_table.shape
    per_b = lambda shape: pl.BlockSpec((1,) + shape, lambda b, pt: (b, 0, 0))
    mask4 = mask.reshape(n_pages + 1, db, 1, LANE)
    return _paged_call(
        functools.partial(_sample_a_kernel, layer=layer, n_pages=n_pages, ch=ch), "sample_a",
        page_table, cache, (rel_bias, q, kv_new, mask4),
        [pl.BlockSpec(memory_space=pltpu.SMEM), per_b((8, HEAD_DIM)), per_b((1, LANE)),
         pl.BlockSpec((n_pages + 1, 1, 1, LANE), lambda b, pt: (0, b, 0, 0))],
        jax.ShapeDtypeStruct((db, 8, LANE), F32), per_b((8, LANE)), ch)


def _split_bf16(x):
    hi = x.astype(BF16)
    return hi, (x - hi.astype(F32)).astype(BF16)


def _sample_b_kernel(pt_ref, q_ref, sfx_ref, cross_ref, cache_ref, o_ref, buf, sem, *, layer, n_pages, ch):
    b, nb = pl.program_id(0), pl.num_programs(0)
    n_chunks = n_pages // ch
    q = q_ref[0]

    def start(seq, c, slot):
        for cp in _page_copies(pt_ref, cache_ref, layer, buf, sem, slot, ch, lambda k: pt_ref[seq, c * ch + k]):
            cp.start()

    def wait(slot):
        for cp in _page_copies(pt_ref, cache_ref, layer, buf, sem, slot, ch, lambda k: 0):
            cp.wait()

    def compute(kv_t, right, acc):
        z = jnp.concatenate(_lane_tiles(_dot(q, kv_t[:2 * HEAD_DIM]), ch), axis=0)
        ls = _log_sigmoid(z)
        hi, lo = _split_bf16(ls - z)
        cum = _dot(hi, sfx_ref[...]) + _dot(lo, sfx_ref[...])
        tot = cum[:, LANE:]
        thi, tlo = _split_bf16(tot)
        off = _dot(cross_ref[...], thi) + _dot(cross_ref[...], tlo)
        w = jnp.exp(ls + cum[:, :LANE] + off + jnp.concatenate([right] * ch, axis=0))
        wl = jnp.concatenate([w[k * 8:(k + 1) * 8] for k in range(ch)], axis=1).astype(BF16)
        return right + off[:8] + tot[:8], acc + _nt_dot(wl, kv_t[2 * HEAD_DIM:])

    @pl.when(b == 0)
    def _():
        start(0, n_chunks - 1, 0)

    @pl.when(b + 1 < nb)
    def _():
        start(b + 1, n_chunks - 1, (b + 1) & 1)

    wait(b & 1)
    zero = jnp.zeros((8, LANE), F32)
    right, acc = compute(_chunk_slab(buf.at[b & 1], ch), zero, zero)

    def more(s):
        c, right, _ = s
        return (c >= 0) & (jnp.max(right) > SURVIVAL_CUTOFF)

    def older(s):
        c, right, acc = s
        start(b, c, 2)
        wait(2)
        right, acc = compute(_chunk_slab(buf.at[2], ch), right, acc)
        return c - 1, right, acc

    _, _, acc = lax.while_loop(more, older, (n_chunks - 2, right, acc))
    o_ref[0] = acc


def _sample_b(page_table, cache, layer, q, sfx, cross, ch):
    db, _ = page_table.shape
    per_b = lambda shape: pl.BlockSpec((1,) + shape, lambda b, pt: (b, 0, 0))
    full2 = lambda a: pl.BlockSpec(a.shape, lambda b, pt: (0, 0))
    return _paged_call(
        functools.partial(_sample_b_kernel, layer=layer, n_pages=page_table.shape[1], ch=ch), "sample_b",
        page_table, cache, (q, sfx, cross), [per_b((8, LANE)), full2(sfx), full2(cross)],
        jax.ShapeDtypeStruct((db, 8, LANE), F32), per_b((8, LANE)), ch, slots=3)


def _sample_c_kernel(pt_ref, q_ref, rown_ref, cache_ref, o_ref, buf, sem, *, layer, n_pages, ch):
    q = q_ref[0]
    width = C_KV_RANK + C_ROPE_DIM

    def compute(c, rows_t, state):
        lg = _dot(q[:, :C_KV_RANK], rows_t[:C_KV_RANK]) + _dot(q[:, C_KV_RANK:width], rows_t[C_KV_RANK:])
        return _softmax_step(lg, rows_t[:C_KV_RANK], state)

    state = _paged_loop(pt_ref, cache_ref, layer, buf, sem, n_pages // ch, ch, compute, _softmax_init())
    row_new = rown_ref[0].astype(F32)
    lg_new = jnp.sum(q.astype(F32) * row_new, axis=1, keepdims=True)
    o_ref[0] = _softmax_finish(lg_new, row_new[:, :C_KV_RANK], state)


def _sample_c(page_table, cache, layer, q, row_new, ch):
    db, _ = page_table.shape
    per_b = lambda shape: pl.BlockSpec((1,) + shape, lambda b, pt: (b, 0, 0))
    return _paged_call(
        functools.partial(_sample_c_kernel, layer=layer, n_pages=page_table.shape[1], ch=ch), "sample_c",
        page_table, cache, (q, row_new), [per_b((8, 2 * LANE)), per_b((1, 2 * LANE))],
        jax.ShapeDtypeStruct((db, 8, LANE), F32), per_b((8, LANE)), ch)


def _c_up_kernel(ol_ref, wuv_ref, o_ref):
    o_ref[...] = jnp.concatenate([_dot(ol_ref[:, h, :].astype(BF16), wuv_ref[h]) for h in range(C_HEADS)], axis=1)


def _c_up(o_lat, wuv):
    db = o_lat.shape[0]
    return pl.pallas_call(
        _c_up_kernel,
        out_shape=jax.ShapeDtypeStruct((db, C_HEADS * HEAD_DIM), F32),
        name="sample_c_up",
    )(o_lat, wuv)


def _cross_page_matrix(ch):
    idx = jnp.arange(ch * 8)
    return ((idx[:, None] % 8 == idx[None, :] % 8) & (idx[None, :] // 8 > idx[:, None] // 8)).astype(BF16)


def _layer_weights(l, w_in, c_q_norm, c_kv_norm, w_uq, w_uk, w_uv, w_out):
    wukt = jnp.transpose(w_uk[l], (1, 2, 0)).astype(BF16)
    wuv = jnp.transpose(w_uv[l], (1, 0, 2)).astype(BF16)
    proj = (_permute_w_in(w_in[l]), c_q_norm[l].reshape(1, -1), c_kv_norm[l].reshape(1, -1),
            _permute_w_uq(w_uq[l]), wukt, _rope_select())
    return proj, wuv, w_out[l].astype(BF16)


def _prompt_layer(hp, t_out, proj_w, wuv, w_out, lg, lb, tabs, rel_bias, pfx, sfx, n_sel, alpha):
    rows = hp.shape[1] // 4
    (aq, akv32, akv16, aiq, aik32, aik16, aiw, gates, bq, bkv32, bkvh, cqf, crow32, crow16) = _project(
        hp, t_out, rows, proj_w, tabs)
    oa = _attn_a(rel_bias, aiq, aiw, aq, aik16, akv16, pfx, n_sel)
    ob = _attn_b(bq, bkvh, sfx)
    oc = _attn_c(cqf, crow16, wuv)
    hp = _merge(hp, gates, oa, ob, oc, w_out, lg, lb, rows, alpha)
    return hp, (akv32, aik32, bkv32, crow32)


def _sample_layer(hs, l, proj_w, wuv, w_out, lg, lb, tabs, rel_bias, pfx, sfx, cross, n_sel, alpha,
                  page_table, cache_a_kv, cache_a_kidx, cache_b_kv, cache_c_kv, ch):
    db = hs.shape[0]
    (aq, akv32, akv16, aiq, aik32, aik16, aiw, gates, bq, bkv32, bkvh, cqf, crow32, crow16) = _project(
        hs[None], db, db, proj_w, tabs)
    by_seq = lambda a: jnp.transpose(a[0], (1, 0, 2))

    scores = _sample_idx(page_table, cache_a_kidx, l, by_seq(aiq), aiw[0, :, IDX_DIM:IDX_DIM + IDX_HEADS, None],
                         aik16[0, :, None, :IDX_DIM], ch)
    mask = _sample_topk(scores, pfx, n_sel)
    qa = jnp.pad(by_seq(aq), ((0, 0), (0, 8 - A_HEADS), (0, 0)))
    oa = _sample_a(page_table, cache_a_kv, l, rel_bias, qa, akv16[0, :, None, :], mask, ch)
    oa = oa[:, :A_HEADS, HEAD_DIM:].reshape(db, A_HEADS * HEAD_DIM)

    qb = by_seq(bq)
    zb = jnp.zeros_like(qb[:, :2])
    qb = jnp.concatenate([jnp.concatenate([qb[:, :2], zb], axis=2), jnp.concatenate([zb, qb[:, 2:]], axis=2),
                          jnp.zeros((db, 8 - B_HEADS, LANE), BF16)], axis=1)
    ob = _sample_b(page_table, cache_b_kv, l, qb, sfx, cross, cross.shape[0] // 8)
    ob = jnp.concatenate([ob[:, 0, :HEAD_DIM], ob[:, 1, :HEAD_DIM], ob[:, 2, HEAD_DIM:], ob[:, 3, HEAD_DIM:]], axis=1)

    o_lat = _sample_c(page_table, cache_c_kv, l, by_seq(cqf), crow16[0, :, None, :], ch)
    oc = _c_up(o_lat, wuv)
    hs = _merge(hs[None], gates, oa[None], ob[None], oc[None], w_out, lg, lb, db, alpha)[0]
    return hs, (akv32[0], aik32[0], bkv32[0], crow32[0])


def kernel(x_prompt, x_sample, cache_a_kv, cache_a_kidx, cache_b_kv, cache_c_kv, page_table, meta_tokens, rel_bias,
           emb_ln_g, emb_ln_b, w_in, c_q_norm, c_kv_norm, w_uq, w_uk, w_uv, w_out, ln_g, ln_b):
    bp, seq, d = x_prompt.shape
    db, dec_seq, _ = x_sample.shape
    assert dec_seq == 1
    depth = w_in.shape[0]
    n_phys = cache_a_kv.shape[0]
    n_pages = page_table.shape[1]
    past = n_pages * PAGE_SIZE
    t_p = N_META + seq
    t_pad = -(-t_p // TILE) * TILE
    n_sel_p = min(TOPK_MAX, seq // 4)
    n_sel_s = min(TOPK_MAX, (past + dec_seq) // 4)
    alpha = (2.0 * depth) ** 0.25
    ch = min(16, n_pages // 2)

    meta = jnp.broadcast_to(meta_tokens[None], (bp, N_META, d))
    xp = jnp.concatenate([meta, x_prompt, jnp.zeros((bp, t_pad - t_p, d), F32)], axis=1).reshape(bp * t_pad, d)
    ln_rows = 4 * TILE if (bp * t_pad) % (4 * TILE) == 0 else TILE
    hp = _layernorm(xp, emb_ln_g, emb_ln_b, ln_rows).reshape(bp, t_pad, d)
    hs = _layernorm(x_sample.reshape(db, d), emb_ln_g, emb_ln_b, db)

    tabs_p = _rope_tables(jnp.arange(t_pad, dtype=I32))
    tabs_s = _rope_tables(jnp.full((db,), past, I32))
    pfx, sfx = _prefix_matrices()
    cross = _cross_page_matrix(min(4, n_pages // 2))
    ca_idx = jnp.transpose(cache_a_kidx, (0, 1, 3, 2))
    ca_kv = jnp.transpose(cache_a_kv, (0, 1, 3, 4, 2)).reshape(n_phys, depth, 2 * HEAD_DIM, PAGE_SIZE)
    cb_kv = jnp.transpose(cache_b_kv, (0, 1, 3, 4, 5, 2)).reshape(n_phys, depth, 4 * HEAD_DIM, PAGE_SIZE)
    cc_kv = jnp.transpose(cache_c_kv, (0, 1, 3, 2))

    rows_p, rows_s = [], []
    for l in range(depth):
        proj_w, wuv, wo = _layer_weights(l, w_in, c_q_norm, c_kv_norm, w_uq, w_uk, w_uv, w_out)
        hp, rp = _prompt_layer(hp, t_p, proj_w, wuv, wo, ln_g[l], ln_b[l], tabs_p, rel_bias, pfx, sfx, n_sel_p, alpha)
        hs, rs = _sample_layer(hs, l, proj_w, wuv, wo, ln_g[l], ln_b[l], tabs_s, rel_bias, pfx, sfx, cross, n_sel_s,
                               alpha, page_table, ca_kv, ca_idx, cb_kv, cc_kv, ch)
        rows_p.append(rp)
        rows_s.append(rs)

    def stack(rows, k, lead, tail):
        return jnp.stack([r[k].reshape(lead + tail) for r in rows], axis=1)

    lp, ls = (bp, t_p), (db, dec_seq)
    return (hp[:, N_META:t_p], hs[:, None, :],
            stack(rows_p, 0, lp, (2, HEAD_DIM)), stack(rows_p, 1, lp, (IDX_DIM,)),
            stack(rows_p, 2, lp, (2, B_KV_HEADS, HEAD_DIM)), stack(rows_p, 3, lp, (C_KV_RANK + C_ROPE_DIM,)),
            stack(rows_s, 0, ls, (2, HEAD_DIM)), stack(rows_s, 1, ls, (IDX_DIM,)),
            stack(rows_s, 2, ls, (2, B_KV_HEADS, HEAD_DIM)), stack(rows_s, 3, ls, (C_KV_RANK + C_ROPE_DIM,)))
```
